```python
import math
import jax, jax.numpy as jnp
from jax import lax
import numpy as np

D_MODEL = 4096
BATCH = 4
SEQ = 2048
DEPTH = 4
DEC_BATCH = 32
DEC_SEQ = 8
PAST_LEN = 8192
PAGE_SIZE = 128

N_MIXERS = 3
N_A_LAYERS = (DEPTH + 2) // 3
N_B_LAYERS = (DEPTH + 1) // 3
N_C_LAYERS = DEPTH // 3

HG_EXPAND = 128
HG_HEADS = D_MODEL // HG_EXPAND
HG_DK = HG_EXPAND
HG_DV = D_MODEL // HG_HEADS
HG_KDIM = HG_HEADS * HG_DK
HG_VDIM = HG_HEADS * HG_DV
HG_CHUNK = 32

SWA_WINDOW = 128
SWA_HEAD_DIM = 64
SWA_HEADS = D_MODEL // SWA_HEAD_DIM
SWA_KV_HEADS = 8
SWA_GROUP = SWA_HEADS // SWA_KV_HEADS
SWA_Q_DIM = SWA_HEADS * SWA_HEAD_DIM
SWA_KV_DIM = SWA_KV_HEADS * SWA_HEAD_DIM
SWA_SCALE = SWA_HEAD_DIM ** -0.5

CONV_WIDTH = 31
CONV_DIM = D_MODEL

MOE_EXPERTS = 64
MOE_TOP_K = 8
MOE_GROUPS = 8
MOE_TOPK_GROUPS = 4
MOE_FF = 768
MOE_SHARED_FF = 768
MOE_ROUTED_SCALE = 2.5
MOE_BLOCK = 128

DN_ALPHA = (2 * DEPTH) ** 0.25
DN_BETA = (8 * DEPTH) ** -0.25
LN_EPS = 1e-5
RMS_EPS = 1e-6

kernel_name = 'hybrid_hgrn2_swa_conformer_moe_step'


def layer_norm(x, g, b):
    xf = x.astype(jnp.float32)
    mu = xf.mean(-1, keepdims=True)
    var = jnp.mean(jnp.square(xf - mu), -1, keepdims=True)
    return ((xf - mu) * lax.rsqrt(var + LN_EPS) * g.astype(jnp.float32) + b.astype(jnp.float32)).astype(x.dtype)


def gated_recurrence(q, k, v, log_f, s0, chunk):
    b, L, h, _ = q.shape
    dv = v.shape[-1]
    n = L // chunk

    def blocks(t):
        return jnp.moveaxis(t.astype(jnp.float32).reshape(b, n, chunk, h, t.shape[-1]), 1, 0)

    causal = jnp.tril(jnp.ones((chunk, chunk), bool))[None, :, :, None, None]

    def step(s, inp):
        qc, kc, vc, gc = inp
        cum = jnp.cumsum(gc, axis=1)
        rel = jnp.where(causal, cum[:, :, None] - cum[:, None], -jnp.inf)
        att = jnp.einsum('btshd,bshd->bhts', qc[:, :, None] * jnp.exp(rel), kc)
        o = jnp.einsum('bhts,bshv->bthv', att, vc) + jnp.einsum('bthd,bhdv->bthv', qc * jnp.exp(cum), s)
        last = cum[:, -1]
        s = jnp.exp(last)[..., None] * s + jnp.einsum('bshd,bshv->bhdv', kc * jnp.exp(last[:, None] - cum), vc)
        return s, o

    s, o = lax.scan(step, s0.astype(jnp.float32), (blocks(q), blocks(k), blocks(v), blocks(log_f)))
    return jnp.moveaxis(o, 0, 1).reshape(b, L, h, dv), s


def hgrn2_mixer(x, s0, w_in, lb, g_norm, w_o):
    b, L, _ = x.shape
    hcat = x @ w_in
    q = jax.nn.silu(hcat[..., :HG_KDIM])
    f = hcat[..., HG_KDIM:2 * HG_KDIM].astype(jnp.float32)
    i = hcat[..., 2 * HG_KDIM:2 * HG_KDIM + HG_VDIM]
    g = hcat[..., 2 * HG_KDIM + HG_VDIM:]
    lb = lb.astype(jnp.float32)
    log_f = jnp.logaddexp(jnp.log(lb), jnp.log1p(-lb) + jax.nn.log_sigmoid(f))
    k = (1.0 - lb) * jax.nn.sigmoid(-f)
    heads_k = lambda t: t.reshape(b, L, HG_HEADS, HG_DK)
    o, s = gated_recurrence(heads_k(q), heads_k(k), i.reshape(b, L, HG_HEADS, HG_DV), heads_k(log_f), s0,
                            math.gcd(L, HG_CHUNK))
    gh = g.reshape(b, L, HG_HEADS, HG_DV).astype(jnp.float32)
    o = o * lax.rsqrt(jnp.mean(jnp.square(o), -1, keepdims=True) + RMS_EPS) * g_norm.astype(jnp.float32) * jax.nn.silu(gh)
    return o.reshape(b, L, HG_VDIM).astype(x.dtype) @ w_o, s.astype(x.dtype)


def swa_project(x, w_qkv, b_qkv):
    b, L, _ = x.shape
    qkv = x @ w_qkv + b_qkv
    q = qkv[..., :SWA_Q_DIM].reshape(b, L, SWA_KV_HEADS, SWA_GROUP, SWA_HEAD_DIM)
    k = qkv[..., SWA_Q_DIM:SWA_Q_DIM + SWA_KV_DIM].reshape(b, L, SWA_KV_HEADS, SWA_HEAD_DIM)
    v = qkv[..., SWA_Q_DIM + SWA_KV_DIM:].reshape(b, L, SWA_KV_HEADS, SWA_HEAD_DIM)
    return q, k, v


def sink_attend(q, k, v, valid, sinks):
    s = jnp.einsum('bnqkgd,bnskd->bnkgqs', q, k).astype(jnp.float32) * SWA_SCALE
    s = jnp.where(valid[None, :, None, None], s, -jnp.inf)
    sink = sinks.astype(jnp.float32).reshape(1, 1, SWA_KV_HEADS, SWA_GROUP, 1, 1)
    m = jnp.maximum(s.max(-1, keepdims=True), sink)
    p = jnp.exp(s - m)
    probs = p / (p.sum(-1, keepdims=True) + jnp.exp(sink - m))
    return jnp.einsum('bnkgqs,bnskd->bnqkgd', probs.astype(v.dtype), v)


def swa_prompt(x, buf_len, w_qkv, b_qkv, sinks, w_o, b_o):
    b, L, _ = x.shape
    nb = L // SWA_WINDOW
    q, k, v = swa_project(x, w_qkv, b_qkv)
    qb = q.reshape(b, nb, SWA_WINDOW, SWA_KV_HEADS, SWA_GROUP, SWA_HEAD_DIM)

    def band(t):
        tb = t.reshape(b, nb, SWA_WINDOW, SWA_KV_HEADS, SWA_HEAD_DIM)
        prev = jnp.concatenate([jnp.zeros_like(tb[:, :1]), tb[:, :-1]], axis=1)
        return jnp.concatenate([prev, tb], axis=2)

    qpos = jnp.arange(nb)[:, None] * SWA_WINDOW + jnp.arange(SWA_WINDOW)[None]
    kpos = (jnp.arange(nb)[:, None] - 1) * SWA_WINDOW + jnp.arange(2 * SWA_WINDOW)[None]
    d = qpos[:, :, None] - kpos[:, None, :]
    valid = (kpos[:, None, :] >= 0) & (d >= 0) & (d < SWA_WINDOW)
    o = sink_attend(qb, band(k), band(v), valid, sinks).reshape(b, L, SWA_Q_DIM)
    return o @ w_o + b_o, k[:, L - buf_len:], v[:, L - buf_len:]


def swa_sample(x, cache_k, cache_v, w_qkv, b_qkv, sinks, w_o, b_o):
    b, L, _ = x.shape
    buf_len = cache_k.shape[1]
    q, k, v = swa_project(x, w_qkv, b_qkv)
    k_all = jnp.concatenate([cache_k.astype(k.dtype), k], axis=1)
    v_all = jnp.concatenate([cache_v.astype(v.dtype), v], axis=1)
    d = jnp.arange(L)[:, None] - (jnp.arange(buf_len + L) - buf_len)[None]
    valid = ((d >= 0) & (d < SWA_WINDOW))[None]
    o = sink_attend(q[:, None], k_all[:, None], v_all[:, None], valid, sinks).reshape(b, L, SWA_Q_DIM)
    return o @ w_o + b_o, k_all[:, L:], v_all[:, L:]


def conv_module(x, buf, w_pw1, b_pw1, w_dw, b_dw, ln_g, ln_b, w_pw2, b_pw2):
    hcat = x @ w_pw1 + b_pw1
    a, gt = jnp.split(hcat, 2, axis=-1)
    u = a * jax.nn.sigmoid(gt)
    full = jnp.concatenate([buf.astype(u.dtype), u], axis=1)
    y = lax.conv_general_dilated(full, w_dw[:, None, :].astype(full.dtype), window_strides=(1,), padding='VALID',
                                 dimension_numbers=('NWC', 'WIO', 'NWC'), feature_group_count=CONV_DIM) + b_dw
    y = jax.nn.silu(layer_norm(y, ln_g, ln_b))
    return y @ w_pw2 + b_pw2, full[:, full.shape[1] - (CONV_WIDTH - 1):]


def swiglu(x, w_gu, w_dn):
    g, u = jnp.split(x @ w_gu, 2, axis=-1)
    return (jax.nn.silu(g) * u) @ w_dn


def routed_experts(x2, eidx, gate, w_gu, w_dn):
    t, d = x2.shape
    n_assign = t * MOE_TOP_K
    n_blk = -(-n_assign // MOE_BLOCK) + MOE_EXPERTS
    n_rows = n_blk * MOE_BLOCK
    flat_e = eidx.reshape(n_assign)
    order = jnp.argsort(flat_e)
    sorted_e = flat_e[order]
    counts = jnp.bincount(flat_e, length=MOE_EXPERTS)
    padded = (counts + MOE_BLOCK - 1) // MOE_BLOCK * MOE_BLOCK
    pad_end = jnp.cumsum(padded)
    start = jnp.cumsum(counts) - counts
    dest = (pad_end - padded)[sorted_e] + jnp.arange(n_assign) - start[sorted_e]
    row_tok = jnp.full((n_rows,), t, jnp.int32).at[dest].set((order // MOE_TOP_K).astype(jnp.int32))
    row_gate = jnp.zeros((n_rows,), jnp.float32).at[dest].set(gate.reshape(n_assign)[order])
    blk_e = jnp.minimum(jnp.searchsorted(pad_end, jnp.arange(n_blk) * MOE_BLOCK, side='right'), MOE_EXPERTS - 1)
    x_rows = jnp.concatenate([x2, jnp.zeros((1, d), x2.dtype)])[row_tok].reshape(n_blk, MOE_BLOCK, d)

    def expert_block(args):
        xb, e = args
        return swiglu(xb, w_gu[e], w_dn[e])

    y_rows = lax.map(expert_block, (x_rows, blk_e)).reshape(n_rows, d)
    y = jax.ops.segment_sum(y_rows * row_gate[:, None].astype(y_rows.dtype), row_tok, num_segments=t + 1)
    return y[:t]


def moe_ffn(x, w_router, b_router, w_gu, w_dn, ws_gu, ws_dn):
    b, L, d = x.shape
    x2 = x.reshape(b * L, d)
    t = x2.shape[0]
    scores = jax.nn.sigmoid(x2.astype(jnp.float32) @ w_router.astype(jnp.float32))
    sel = scores + b_router.astype(jnp.float32)
    grp = lax.top_k(sel.reshape(t, MOE_GROUPS, MOE_EXPERTS // MOE_GROUPS), 2)[0].sum(-1)
    _, gidx = lax.top_k(grp, MOE_TOPK_GROUPS)
    gmask = jax.nn.one_hot(gidx, MOE_GROUPS).sum(-2) > 0
    emask = jnp.repeat(gmask, MOE_EXPERTS // MOE_GROUPS, axis=-1)
    _, eidx = lax.top_k(jnp.where(emask, sel, -jnp.inf), MOE_TOP_K)
    w = jnp.take_along_axis(scores, eidx, axis=-1)
    w = w / w.sum(-1, keepdims=True) * MOE_ROUTED_SCALE
    y = routed_experts(x2, eidx, w, w_gu, w_dn) + swiglu(x2, ws_gu, ws_dn)
    return y.reshape(b, L, d)


def setup_inputs(seed: int = 0) -> dict:
    key = jax.random.key(seed)
    keys = iter(jax.random.split(key, 40))

    def nrm(shape, scale):
        return jax.random.normal(next(keys), shape, jnp.float32) * scale

    buf_len = min(SWA_WINDOW, PAST_LEN)
    d = D_MODEL
    return {
        'x_prompt': nrm((BATCH, SEQ, d), 1.0),
        'x_sample': nrm((DEC_BATCH, DEC_SEQ, d), 1.0),
        'state_hgrn': nrm((N_A_LAYERS, DEC_BATCH, HG_HEADS, HG_DK, HG_DV), 0.5),
        'cache_swa_k': nrm((N_B_LAYERS, DEC_BATCH, buf_len, SWA_KV_HEADS, SWA_HEAD_DIM), 1.0),
        'cache_swa_v': nrm((N_B_LAYERS, DEC_BATCH, buf_len, SWA_KV_HEADS, SWA_HEAD_DIM), 1.0),
        'state_conv': nrm((N_C_LAYERS, DEC_BATCH, CONV_WIDTH - 1, CONV_DIM), 0.5),
        'hg_w_in': nrm((N_A_LAYERS, d, 2 * HG_KDIM + 2 * HG_VDIM), d ** -0.5),
        'hg_lb': nrm((N_A_LAYERS, HG_KDIM), 0.1),
        'hg_norm': 1.0 + nrm((N_A_LAYERS, HG_DV), 0.02),
        'hg_w_o': nrm((N_A_LAYERS, HG_VDIM, d), HG_VDIM ** -0.5 * DN_BETA),
        'swa_w_qkv': nrm((N_B_LAYERS, d, SWA_Q_DIM + 2 * SWA_KV_DIM), d ** -0.5),
        'swa_b_qkv': nrm((N_B_LAYERS, SWA_Q_DIM + 2 * SWA_KV_DIM), 0.02),
        'swa_sinks': nrm((N_B_LAYERS, SWA_HEADS), 0.5),
        'swa_w_o': nrm((N_B_LAYERS, SWA_Q_DIM, d), SWA_Q_DIM ** -0.5 * DN_BETA),
        'swa_b_o': nrm((N_B_LAYERS, d), 0.02),
        'cv_w_pw1': nrm((N_C_LAYERS, d, 2 * CONV_DIM), d ** -0.5),
        'cv_b_pw1': nrm((N_C_LAYERS, 2 * CONV_DIM), 0.02),
        'cv_w_dw': nrm((N_C_LAYERS, CONV_WIDTH, CONV_DIM), CONV_WIDTH ** -0.5),
        'cv_b_dw': nrm((N_C_LAYERS, CONV_DIM), 0.02),
        'cv_ln_g': 1.0 + nrm((N_C_LAYERS, CONV_DIM), 0.02),
        'cv_ln_b': nrm((N_C_LAYERS, CONV_DIM), 0.02),
        'cv_w_pw2': nrm((N_C_LAYERS, CONV_DIM, d), CONV_DIM ** -0.5 * DN_BETA),
        'cv_b_pw2': nrm((N_C_LAYERS, d), 0.02),
        'ln_mix_g': 1.0 + nrm((DEPTH, d), 0.02),
        'ln_mix_b': nrm((DEPTH, d), 0.02),
        'ln_ffn_g': 1.0 + nrm((DEPTH, d), 0.02),
        'ln_ffn_b': nrm((DEPTH, d), 0.02),
        'moe_w_router': nrm((DEPTH, d, MOE_EXPERTS), d ** -0.5),
        'moe_b_router': nrm((DEPTH, MOE_EXPERTS), 0.01),
        'moe_w_gu': nrm((DEPTH, MOE_EXPERTS, d, 2 * MOE_FF), d ** -0.5),
        'moe_w_dn': nrm((DEPTH, MOE_EXPERTS, MOE_FF, d), MOE_FF ** -0.5 * DN_BETA),
        'moe_ws_gu': nrm((DEPTH, d, 2 * MOE_SHARED_FF), d ** -0.5),
        'moe_ws_dn': nrm((DEPTH, MOE_SHARED_FF, d), MOE_SHARED_FF ** -0.5 * DN_BETA),
    }


def reference(x_prompt, x_sample, state_hgrn, cache_swa_k, cache_swa_v, state_conv,
              hg_w_in, hg_lb, hg_norm, hg_w_o,
              swa_w_qkv, swa_b_qkv, swa_sinks, swa_w_o, swa_b_o,
              cv_w_pw1, cv_b_pw1, cv_w_dw, cv_b_dw, cv_ln_g, cv_ln_b, cv_w_pw2, cv_b_pw2,
              ln_mix_g, ln_mix_b, ln_ffn_g, ln_ffn_b,
              moe_w_router, moe_b_router, moe_w_gu, moe_w_dn, moe_ws_gu, moe_ws_dn):
    lb_p = jax.nn.softmax(hg_lb.astype(jnp.float32), axis=0)
    lb_all = jnp.cumsum(lb_p, axis=0) - lb_p[0]
    buf_len = cache_swa_k.shape[2]
    xp, xs = x_prompt, x_sample
    hg_p, hg_s, k_p, k_s, v_p, v_s, cv_p, cv_s = [], [], [], [], [], [], [], []
    for layer in range(DEPTH):
        kind, j = layer % N_MIXERS, layer // N_MIXERS
        if kind == 0:
            s0 = jnp.zeros((xp.shape[0], HG_HEADS, HG_DK, HG_DV), jnp.float32)
            mp, sp = hgrn2_mixer(xp, s0, hg_w_in[j], lb_all[j], hg_norm[j], hg_w_o[j])
            ms, ss = hgrn2_mixer(xs, state_hgrn[j], hg_w_in[j], lb_all[j], hg_norm[j], hg_w_o[j])
            hg_p.append(sp)
            hg_s.append(ss)
        elif kind == 1:
            mp, kp_, vp_ = swa_prompt(xp, buf_len, swa_w_qkv[j], swa_b_qkv[j], swa_sinks[j], swa_w_o[j], swa_b_o[j])
            ms, ks_, vs_ = swa_sample(xs, cache_swa_k[j], cache_swa_v[j], swa_w_qkv[j], swa_b_qkv[j], swa_sinks[j],
                                      swa_w_o[j], swa_b_o[j])
            k_p.append(kp_)
            v_p.append(vp_)
            k_s.append(ks_)
            v_s.append(vs_)
        else:
            buf0 = jnp.zeros((xp.shape[0], CONV_WIDTH - 1, CONV_DIM), xp.dtype)
            cargs = (cv_w_pw1[j], cv_b_pw1[j], cv_w_dw[j], cv_b_dw[j], cv_ln_g[j], cv_ln_b[j], cv_w_pw2[j], cv_b_pw2[j])
            mp, bp = conv_module(xp, buf0, *cargs)
            ms, bs = conv_module(xs, state_conv[j], *cargs)
            cv_p.append(bp)
            cv_s.append(bs)
        xp = layer_norm(DN_ALPHA * xp + mp, ln_mix_g[layer], ln_mix_b[layer])
        xs = layer_norm(DN_ALPHA * xs + ms, ln_mix_g[layer], ln_mix_b[layer])
        margs = (moe_w_router[layer], moe_b_router[layer], moe_w_gu[layer], moe_w_dn[layer], moe_ws_gu[layer], moe_ws_dn[layer])
        xp = layer_norm(DN_ALPHA * xp + moe_ffn(xp, *margs), ln_ffn_g[layer], ln_ffn_b[layer])
        xs = layer_norm(DN_ALPHA * xs + moe_ffn(xs, *margs), ln_ffn_g[layer], ln_ffn_b[layer])
    return (xp, xs, jnp.stack(hg_p), jnp.stack(hg_s), jnp.stack(k_p), jnp.stack(k_s), jnp.stack(v_p), jnp.stack(v_s),
            jnp.stack(cv_p), jnp.stack(cv_s))
```

```python
import functools

import jax
import jax.numpy as jnp
from jax import lax
from jax.experimental import pallas as pl
from jax.experimental.pallas import tpu as pltpu

F32 = jnp.float32
BF16 = jnp.bfloat16

N_MIXERS = 3
HG_EXPAND = 128
HG_CHUNK = 32
SWA_WINDOW = 128
SWA_HEAD_DIM = 64
SWA_KV_HEADS = 8
MOE_TOP_K = 8
MOE_GROUPS = 8
MOE_TOPK_GROUPS = 4
MOE_ROUTED_SCALE = 2.5
LN_EPS = 1e-5
RMS_EPS = 1e-6

LANES = 128
SUBLANES = 8
VMEM_LIMIT_BYTES = 56 * 1024 * 1024


def _tile(dim, prefs):
    for p in prefs:
        if dim % p == 0:
            return p
    return dim


def _params(sem):
    return pltpu.CompilerParams(dimension_semantics=sem, vmem_limit_bytes=VMEM_LIMIT_BYTES)


def _sigmoid(x):
    return 1.0 / (1.0 + jnp.exp(-x))


def _mm_kernel(*refs, has_bias):
    if has_bias:
        x_ref, w_ref, b_ref, o_ref, wbf_ref = refs
    else:
        x_ref, w_ref, o_ref, wbf_ref = refs

    @pl.when(pl.program_id(1) == 0)
    def _():
        wbf_ref[...] = w_ref[...].astype(BF16)

    acc = jnp.dot(x_ref[...].astype(BF16), wbf_ref[...], preferred_element_type=F32)
    if has_bias:
        acc = acc + b_ref[...]
    o_ref[...] = acc.astype(o_ref.dtype)


def matmul(x, w, b=None, out_dtype=F32):
    m, k = x.shape
    n = w.shape[1]
    tm = _tile(m, (768, 512, 256, 128, 64, 32, 16, 8))
    tn = _tile(n, (512, 256, 128))
    in_specs = [pl.BlockSpec((tm, k), lambda j, i: (i, 0)),
                pl.BlockSpec((k, tn), lambda j, i: (0, j))]
    args = [x, w]
    if b is not None:
        in_specs.append(pl.BlockSpec((1, tn), lambda j, i: (0, j)))
        args.append(b.reshape(1, n))
    return pl.pallas_call(
        functools.partial(_mm_kernel, has_bias=b is not None),
        out_shape=jax.ShapeDtypeStruct((m, n), out_dtype),
        grid=(n // tn, m // tm),
        in_specs=in_specs,
        out_specs=pl.BlockSpec((tm, tn), lambda j, i: (i, j)),
        scratch_shapes=[pltpu.VMEM((k, tn), BF16)],
        compiler_params=_params(("arbitrary", "arbitrary")),
        name="matmul",
    )(*args)


def _layer_norm_rows(z, g, b):
    mu = jnp.mean(z, axis=-1, keepdims=True)
    zc = z - mu
    var = jnp.mean(zc * zc, axis=-1, keepdims=True)
    return zc * lax.rsqrt(var + LN_EPS) * g + b


def _ln_res_kernel(x_ref, m_ref, g_ref, b_ref, o_ref, obf_ref, *, alpha):
    y = _layer_norm_rows(alpha * x_ref[...] + m_ref[...], g_ref[...], b_ref[...])
    o_ref[...] = y
    obf_ref[...] = y.astype(BF16)


def ln_residual(x, m, g, b, alpha):
    t, d = x.shape
    tm = _tile(t, (256, 128, 64, 32, 16))
    row = pl.BlockSpec((tm, d), lambda i: (i, 0))
    vec = pl.BlockSpec((1, d), lambda i: (0, 0))
    return pl.pallas_call(
        functools.partial(_ln_res_kernel, alpha=alpha),
        out_shape=(jax.ShapeDtypeStruct((t, d), F32), jax.ShapeDtypeStruct((t, d), BF16)),
        grid=(t // tm,),
        in_specs=[row, row, vec, vec],
        out_specs=(row, row),
        compiler_params=_params(("arbitrary",)),
        name="ln_residual",
    )(x, m, g.reshape(1, d), b.reshape(1, d))


def _hgrn_kernel(*refs, chunk, n_chunks, has_s0):
    if has_s0:
        q_ref, f_ref, i_ref, g_ref, lb_ref, gn_ref, s0_ref, o_ref, sout_ref, st_ref = refs
    else:
        q_ref, f_ref, i_ref, g_ref, lb_ref, gn_ref, o_ref, sout_ref, st_ref = refs
    c = pl.program_id(2)

    @pl.when(c == 0)
    def _():
        if has_s0:
            st_ref[...] = s0_ref[...].T
        else:
            st_ref[...] = jnp.zeros_like(st_ref)

    lb = lb_ref[...]
    log_lb = jnp.log(lb)
    log_1m_lb = jnp.log1p(-lb)
    one_m_lb = 1.0 - lb
    gn = gn_ref[...]
    dk = lb.shape[-1]
    n_sub = max(chunk // SUBLANES, 1)
    sub = chunk // n_sub
    tril = (lax.broadcasted_iota(jnp.int32, (chunk, chunk), 0)
            >= lax.broadcasted_iota(jnp.int32, (chunk, chunk), 1)).astype(F32)
    lane_t = lax.broadcasted_iota(jnp.int32, (sub, chunk), 1)
    s_in_sub = lax.broadcasted_iota(jnp.int32, (sub, dk), 0)

    def body(ci, carry):
        r0 = pl.multiple_of(ci * chunk, chunk)
        hq = q_ref[pl.ds(r0, chunk), :]
        hf = f_ref[pl.ds(r0, chunk), :]
        v = i_ref[pl.ds(r0, chunk), :]
        hg = g_ref[pl.ds(r0, chunk), :]
        q = hq * _sigmoid(hq)
        log_sig = jnp.minimum(hf, 0.0) - jnp.log1p(jnp.exp(-jnp.abs(hf)))
        bt = log_1m_lb + log_sig
        log_f = jnp.maximum(log_lb, bt) + jnp.log1p(jnp.exp(-jnp.abs(log_lb - bt)))
        k = one_m_lb * _sigmoid(-hf)
        cum = jnp.dot(tril, log_f, precision=lax.Precision.HIGHEST, preferred_element_type=F32)
        last = cum[chunk - 1:chunk, :]

        att_t = [jnp.zeros((sub, chunk), F32) for _ in range(n_sub)]
        for t in range(chunk):
            jt = t // sub
            qt = q[t:t + 1, :]
            ct = cum[t:t + 1, :]
            for j in range(jt + 1):
                rel = ct - cum[j * sub:(j + 1) * sub, :]
                if j == jt:
                    rel = jnp.where(s_in_sub <= (t - jt * sub), rel, -jnp.inf)
                p = (qt * jnp.exp(rel)) * k[j * sub:(j + 1) * sub, :]
                r = jnp.sum(p, axis=-1, keepdims=True)
                att_t[j] = jnp.where(lane_t == t, r, att_t[j])
        att_t = att_t[0] if n_sub == 1 else jnp.concatenate(att_t, axis=0)

        st = st_ref[...]
        qe = (q * jnp.exp(cum)).astype(BF16)
        o = lax.dot_general(qe, st.astype(BF16), (((1,), (1,)), ((), ())), preferred_element_type=F32)
        vb = v.astype(BF16)
        o = o + lax.dot_general(att_t.astype(BF16), vb, (((0,), (0,)), ((), ())),
                                preferred_element_type=F32)
        kdec = (k * jnp.exp(last - cum)).astype(BF16)
        st_ref[...] = jnp.exp(last) * st + lax.dot_general(
            vb, kdec, (((0,), (0,)), ((), ())), preferred_element_type=F32)

        o = o * lax.rsqrt(jnp.mean(o * o, axis=-1, keepdims=True) + RMS_EPS) * gn * (hg * _sigmoid(hg))
        o_ref[pl.ds(r0, chunk), :] = o.astype(o_ref.dtype)
        return carry

    lax.fori_loop(0, n_chunks, body, 0)

    @pl.when(c == pl.num_programs(2) - 1)
    def _():
        sout_ref[...] = st_ref[...].T


def hgrn_recurrence(hcat, lb, gnorm, s0, o_prev, *, row0, nb, seq, chunk):
    t, d4 = hcat.shape
    d = d4 // 4
    hd = HG_EXPAND
    nh = d // hd
    rb = _tile(seq, (512, 256, 128, 64, 32, 16, 8))
    rb = max(rb, chunk)
    nrb = seq // rb
    rblk0 = row0 // rb

    def col(off):
        return pl.BlockSpec((rb, hd), lambda b, h, c, off=off: (rblk0 + b * nrb + c, off + h))

    in_specs = [col(0), col(nh), col(2 * nh), col(3 * nh),
                pl.BlockSpec((1, hd), lambda b, h, c: (0, h)),
                pl.BlockSpec((1, hd), lambda b, h, c: (0, 0))]
    args = [hcat, hcat, hcat, hcat, lb.reshape(1, d), gnorm.reshape(1, hd)]
    if s0 is not None:
        in_specs.append(pl.BlockSpec((None, None, hd, hd), lambda b, h, c: (b, h, 0, 0)))
        args.append(s0)
    aliases = {}
    if o_prev is not None:
        in_specs.append(pl.BlockSpec(memory_space=pl.ANY))
        args.append(o_prev)
        aliases = {len(args) - 1: 0}

    def kern(*refs):
        if o_prev is not None:
            n_in = len(args)
            refs = refs[:n_in - 1] + refs[n_in:]
        _hgrn_kernel(*refs, chunk=chunk, n_chunks=rb // chunk, has_s0=s0 is not None)

    return pl.pallas_call(
        kern,
        out_shape=(jax.ShapeDtypeStruct((t, d), F32), jax.ShapeDtypeStruct((nb, nh, hd, hd), F32)),
        grid=(nb, nh, nrb),
        in_specs=in_specs,
        out_specs=(pl.BlockSpec((rb, hd), lambda b, h, c: (rblk0 + b * nrb + c, h)),
                   pl.BlockSpec((None, None, hd, hd), lambda b, h, c: (b, h, 0, 0))),
        scratch_shapes=[pltpu.VMEM((hd, hd), F32)],
        input_output_aliases=aliases,
        compiler_params=_params(("arbitrary", "arbitrary", "arbitrary")),
        name="hgrn_recurrence",
    )(*args)


def _swa_kernel(q_ref, kp_ref, ko_ref, vp_ref, vo_ref, sink_ref, o_ref, *, group, first_block_has_no_prev):
    lq = q_ref.shape[0]
    sp_len = kp_ref.shape[0]
    so_len = ko_ref.shape[0]
    hd = SWA_HEAD_DIM
    scale = hd ** -0.5
    i_p = lax.broadcasted_iota(jnp.int32, (lq, sp_len), 0)
    j_p = lax.broadcasted_iota(jnp.int32, (lq, sp_len), 1)
    mask_p = j_p > i_p
    if first_block_has_no_prev:
        mask_p = jnp.logical_and(mask_p, pl.program_id(1) > 0)
    i_o = lax.broadcasted_iota(jnp.int32, (lq, so_len), 0)
    j_o = lax.broadcasted_iota(jnp.int32, (lq, so_len), 1)
    mask_o = j_o <= i_o
    nt = (((1,), (1,)), ((), ()))
    kv_per_block = kp_ref.shape[1] // hd
    for kv in range(kv_per_block):
        cs = slice(kv * hd, (kv + 1) * hd)
        kp = kp_ref[:, cs].astype(BF16)
        ko = ko_ref[:, cs].astype(BF16)
        vp = vp_ref[:, cs].astype(BF16)
        vo = vo_ref[:, cs].astype(BF16)
        for g in range(group):
            h = kv * group + g
            qh = q_ref[:, h * hd:(h + 1) * hd].astype(BF16)
            s_p = lax.dot_general(qh, kp, nt, preferred_element_type=F32) * scale
            s_o = lax.dot_general(qh, ko, nt, preferred_element_type=F32) * scale
            s_p = jnp.where(mask_p, s_p, -jnp.inf)
            s_o = jnp.where(mask_o, s_o, -jnp.inf)
            sink = sink_ref[0:1, h:h + 1]
            m = jnp.maximum(jnp.maximum(jnp.max(s_p, axis=-1, keepdims=True),
                                        jnp.max(s_o, axis=-1, keepdims=True)), sink)
            p_p = jnp.exp(s_p - m)
            p_o = jnp.exp(s_o - m)
            den = (jnp.sum(p_p, axis=-1, keepdims=True) + jnp.sum(p_o, axis=-1, keepdims=True)
                   + jnp.exp(sink - m))
            o = (jnp.dot((p_p / den).astype(BF16), vp, preferred_element_type=F32)
                 + jnp.dot((p_o / den).astype(BF16), vo, preferred_element_type=F32))
            o_ref[:, h * hd:(h + 1) * hd] = o.astype(o_ref.dtype)


def swa_attention(qkv, k_prev, v_prev, sinks, o_prev, *, row0, nb, seq, d):
    t = qkv.shape[0]
    hd = SWA_HEAD_DIM
    group = d // hd // SWA_KV_HEADS
    kvb = LANES // hd
    n_kvb = SWA_KV_HEADS // kvb
    qw = kvb * group * hd
    lq = min(seq, SWA_WINDOW)
    nq = seq // lq
    rblk0 = row0 // lq
    kcol0 = d // LANES
    vcol0 = (d + SWA_KV_HEADS * hd) // LANES
    is_prompt = k_prev is None
    own = lambda c0: pl.BlockSpec((lq, LANES), lambda b, n, p: (rblk0 + b * nq + n, c0 + p))
    if is_prompt:
        prev = lambda c0: pl.BlockSpec(
            (lq, LANES), lambda b, n, p: (rblk0 + b * nq + jnp.maximum(n - 1, 0), c0 + p))
        kp_spec, vp_spec = prev(kcol0), prev(vcol0)
        k_prev = v_prev = qkv
    else:
        kp_spec = vp_spec = pl.BlockSpec((SWA_WINDOW, LANES), lambda b, n, p: (b, p))
    in_specs = [pl.BlockSpec((lq, qw), lambda b, n, p: (rblk0 + b * nq + n, p)),
                kp_spec, own(kcol0), vp_spec, own(vcol0),
                pl.BlockSpec((None, 1, kvb * group), lambda b, n, p: (p, 0, 0))]
    args = [qkv, k_prev, qkv, v_prev, qkv, sinks.reshape(n_kvb, 1, kvb * group)]
    aliases = {}
    if o_prev is not None:
        in_specs.append(pl.BlockSpec(memory_space=pl.ANY))
        args.append(o_prev)
        aliases = {len(args) - 1: 0}

    def kern(*refs):
        if o_prev is not None:
            n_in = len(args)
            refs = refs[:n_in - 1] + refs[n_in:]
        _swa_kernel(*refs, group=group, first_block_has_no_prev=is_prompt)

    return pl.pallas_call(
        kern,
        out_shape=jax.ShapeDtypeStruct((t, d), F32),
        grid=(nb, nq, n_kvb),
        in_specs=in_specs,
        out_specs=pl.BlockSpec((lq, qw), lambda b, n, p: (rblk0 + b * nq + n, p)),
        input_output_aliases=aliases,
        compiler_params=_params(("arbitrary", "arbitrary", "arbitrary")),
        name="swa_attention",
    )(*args)


CONV_HALO = 32


def _conv_kernel(*refs, width, raw_halo):
    if raw_halo:
        a_ref, gt_ref, ha_ref, hgt_ref, wdw_ref, bdw_ref, lng_ref, lnb_ref, y_ref, u_ref, full_ref, acc_ref = refs
    else:
        a_ref, gt_ref, hu_ref, wdw_ref, bdw_ref, lng_ref, lnb_ref, y_ref, u_ref, full_ref, acc_ref = refs
    rb, c = a_ref.shape
    u = a_ref[...] * _sigmoid(gt_ref[...])
    u_ref[...] = u
    if raw_halo:
        hu = ha_ref[...] * _sigmoid(hgt_ref[...])
        hu = jnp.where(pl.program_id(1) > 0, hu, 0.0)
    else:
        hu = hu_ref[...]
    full_ref[0:CONV_HALO, :] = hu
    full_ref[CONV_HALO:CONV_HALO + rb, :] = u
    lead = CONV_HALO - (width - 1)
    rt = min(rb, 64)

    def col_tile(ct, carry):
        c0 = pl.multiple_of(ct * LANES, LANES)
        for r0 in range(0, rb, rt):
            acc = jnp.broadcast_to(bdw_ref[:, pl.ds(c0, LANES)], (rt, LANES))
            for w in range(width):
                acc = acc + full_ref[r0 + lead + w:r0 + lead + w + rt, pl.ds(c0, LANES)] * wdw_ref[w:w + 1, pl.ds(c0, LANES)]
            acc_ref[r0:r0 + rt, pl.ds(c0, LANES)] = acc
        return carry

    lax.fori_loop(0, c // LANES, col_tile, 0)
    y = _layer_norm_rows(acc_ref[...], lng_ref[...], lnb_ref[...])
    y_ref[...] = (y * _sigmoid(y)).astype(y_ref.dtype)


def conv_block(hcat, hist, w_dw, b_dw, ln_g, ln_b, y_prev, u_prev, *, row0, nb, seq):
    t, c2 = hcat.shape
    c = c2 // 2
    width = w_dw.shape[0]
    rb = _tile(seq, (256, 128, 64, 32, 16, 8))
    nrb = seq // rb
    rblk0 = row0 // rb
    raw_halo = hist is None
    main = lambda half: pl.BlockSpec((rb, c), lambda b, n: (rblk0 + b * nrb + n, half))
    vec = pl.BlockSpec((1, c), lambda b, n: (0, 0))
    in_specs = [main(0), main(1)]
    args = [hcat, hcat]
    if raw_halo:
        hpb = rb // CONV_HALO
        halo = lambda half: pl.BlockSpec(
            (CONV_HALO, c), lambda b, n: (jnp.maximum((rblk0 + b * nrb + n) * hpb - 1, 0), half))
        in_specs += [halo(0), halo(1)]
        args += [hcat, hcat]
    else:
        in_specs.append(pl.BlockSpec((None, CONV_HALO, c), lambda b, n: (b, 0, 0)))
        args.append(hist)
    in_specs += [pl.BlockSpec((width, c), lambda b, n: (0, 0)), vec, vec, vec]
    args += [w_dw, b_dw.reshape(1, c), ln_g.reshape(1, c), ln_b.reshape(1, c)]
    n_real = len(args)
    aliases = {}
    if y_prev is not None:
        in_specs += [pl.BlockSpec(memory_space=pl.ANY), pl.BlockSpec(memory_space=pl.ANY)]
        args += [y_prev, u_prev]
        aliases = {n_real: 0, n_real + 1: 1}

    def kern(*refs):
        if y_prev is not None:
            refs = refs[:n_real] + refs[n_real + 2:]
        _conv_kernel(*refs, width=width, raw_halo=raw_halo)

    out_blk = pl.BlockSpec((rb, c), lambda b, n: (rblk0 + b * nrb + n, 0))
    return pl.pallas_call(
        kern,
        out_shape=(jax.ShapeDtypeStruct((t, c), F32), jax.ShapeDtypeStruct((t, c), F32)),
        grid=(nb, nrb),
        in_specs=in_specs,
        out_specs=(out_blk, out_blk),
        scratch_shapes=[pltpu.VMEM((CONV_HALO + rb, c), F32), pltpu.VMEM((rb, c), F32)],
        input_output_aliases=aliases,
        compiler_params=_params(("arbitrary", "arbitrary")),
        name="conv_block",
    )(*args)


def _router_kernel(x_ref, w_ref, b_ref, eidx_ref, gate_ref):
    logits = jnp.dot(x_ref[...], w_ref[...], precision=lax.Precision.HIGHEST, preferred_element_type=F32)
    scores = _sigmoid(logits)
    sel = scores + b_ref[...]
    tm, ne = sel.shape
    gsz = ne // MOE_GROUPS
    lane = lax.broadcasted_iota(jnp.int32, (tm, ne), 1).astype(F32)
    grp = lax.broadcasted_iota(jnp.int32, (tm, ne), 1) // gsz
    grp_f = grp.astype(F32)
    neg = -jnp.inf

    def first_argmax(v, idx_f):
        m = jnp.max(v, axis=-1, keepdims=True)
        return m, jnp.min(jnp.where(v == m, idx_f, float(ne)), axis=-1, keepdims=True)

    gscore = jnp.zeros((tm, ne), F32)
    for gi in range(MOE_GROUPS):
        in_g = grp == gi
        v = jnp.where(in_g, sel, neg)
        m1, i1 = first_argmax(v, lane)
        m2 = jnp.max(jnp.where(lane == i1, neg, v), axis=-1, keepdims=True)
        gscore = jnp.where(in_g, m1 + m2, gscore)
    gmask = jnp.zeros((tm, ne), jnp.bool_)
    for _ in range(MOE_TOPK_GROUPS):
        _, gi = first_argmax(gscore, grp_f)
        hit = grp_f == gi
        gmask = jnp.logical_or(gmask, hit)
        gscore = jnp.where(hit, neg, gscore)
    cand = jnp.where(gmask, sel, neg)
    k_iota = lax.broadcasted_iota(jnp.int32, (tm, MOE_TOP_K), 1)
    eidx = jnp.zeros((tm, MOE_TOP_K), F32)
    gate = jnp.zeros((tm, MOE_TOP_K), F32)
    for kk in range(MOE_TOP_K):
        _, ei = first_argmax(cand, lane)
        hit = lane == ei
        wk = jnp.sum(jnp.where(hit, scores, 0.0), axis=-1, keepdims=True)
        cand = jnp.where(hit, neg, cand)
        eidx = jnp.where(k_iota == kk, ei, eidx)
        gate = jnp.where(k_iota == kk, wk, gate)
    gate = gate / jnp.sum(gate, axis=-1, keepdims=True) * MOE_ROUTED_SCALE
    eidx_ref[...] = eidx.astype(jnp.int32)
    gate_ref[...] = gate


def moe_router(x, w_router, b_router):
    t, d = x.shape
    ne = w_router.shape[1]
    tm = _tile(t, (256, 128, 64, 32, 16, 8))
    return pl.pallas_call(
        _router_kernel,
        out_shape=(jax.ShapeDtypeStruct((t, MOE_TOP_K), jnp.int32), jax.ShapeDtypeStruct((t, MOE_TOP_K), F32)),
        grid=(t // tm,),
        in_specs=[pl.BlockSpec((tm, d), lambda i: (i, 0)),
                  pl.BlockSpec((d, ne), lambda i: (0, 0)),
                  pl.BlockSpec((1, ne), lambda i: (0, 0))],
        out_specs=(pl.BlockSpec((tm, MOE_TOP_K), lambda i: (i, 0)),
                   pl.BlockSpec((tm, MOE_TOP_K), lambda i: (i, 0))),
        compiler_params=_params(("arbitrary",)),
        name="moe_router",
    )(x, w_router, b_router.reshape(1, ne))


def _gather_kernel(nvalid_ref, tok_ref, x_hbm, o_ref, sem):
    rows = o_ref.shape[0]

    def row_copy(r, src):
        return pltpu.make_async_copy(x_hbm.at[pl.ds(src, 1), :], o_ref.at[pl.ds(r, 1), :], sem)

    @pl.when(pl.program_id(0) < nvalid_ref[0])
    def _():
        def issue(r, carry):
            row_copy(r, tok_ref[0, r]).start()
            return carry

        lax.fori_loop(0, rows, issue, 0)

        def drain(r, carry):
            row_copy(r, 0).wait()
            return carry

        lax.fori_loop(0, rows, drain, 0)


def gather_rows(x, row_tok, n_valid_blocks, tm):
    t, d = x.shape
    n_rows = row_tok.shape[0]
    n_blk = n_rows // tm
    grid_spec = pltpu.PrefetchScalarGridSpec(
        num_scalar_prefetch=1,
        grid=(n_blk,),
        in_specs=[pl.BlockSpec((None, 1, tm), lambda i, nv: (i, 0, 0), memory_space=pltpu.SMEM),
                  pl.BlockSpec(memory_space=pl.ANY)],
        out_specs=pl.BlockSpec((tm, d), lambda i, nv: (i, 0)),
        scratch_shapes=[pltpu.SemaphoreType.DMA(())],
    )
    return pl.pallas_call(
        _gather_kernel,
        out_shape=jax.ShapeDtypeStruct((n_rows, d), x.dtype),
        grid_spec=grid_spec,
        compiler_params=_params(("arbitrary",)),
        name="moe_gather",
    )(n_valid_blocks.reshape(1), row_tok.reshape(n_blk, 1, tm), x)


def _expert_gu_kernel(e_ref, f_ref, blk_ref, first_ref, valid_ref, x_ref, wg_ref, wu_ref, h_ref, wgb_ref, wub_ref):
    i = pl.program_id(0)

    @pl.when(jnp.logical_and(valid_ref[i] == 1, first_ref[i] == 1))
    def _():
        wgb_ref[...] = wg_ref[...].astype(BF16)
        wub_ref[...] = wu_ref[...].astype(BF16)

    @pl.when(valid_ref[i] == 1)
    def _():
        x = x_ref[...].astype(BF16)
        g = jnp.dot(x, wgb_ref[...], preferred_element_type=F32)
        u = jnp.dot(x, wub_ref[...], preferred_element_type=F32)
        h_ref[...] = (g * _sigmoid(g) * u).astype(h_ref.dtype)


def expert_gate_up(x_rows, w_gu, item_e, item_f, item_blk, item_first, item_valid, tm, tf):
    n_rows, d = x_rows.shape
    ff = w_gu.shape[2] // 2
    n_items = item_e.shape[0]
    nf = ff // tf
    grid_spec = pltpu.PrefetchScalarGridSpec(
        num_scalar_prefetch=5,
        grid=(n_items,),
        in_specs=[pl.BlockSpec((tm, d), lambda i, e, f, b, fi, va: (b[i], 0)),
                  pl.BlockSpec((None, d, tf), lambda i, e, f, b, fi, va: (e[i], 0, f[i])),
                  pl.BlockSpec((None, d, tf), lambda i, e, f, b, fi, va: (e[i], 0, nf + f[i]))],
        out_specs=pl.BlockSpec((tm, tf), lambda i, e, f, b, fi, va: (b[i], f[i])),
        scratch_shapes=[pltpu.VMEM((d, tf), BF16), pltpu.VMEM((d, tf), BF16)],
    )
    return pl.pallas_call(
        _expert_gu_kernel,
        out_shape=jax.ShapeDtypeStruct((n_rows, ff), BF16),
        grid_spec=grid_spec,
        compiler_params=_params(("arbitrary",)),
        name="expert_gate_up",
    )(item_e, item_f, item_blk, item_first, item_valid, x_rows, w_gu, w_gu)


def _expert_dn_kernel(e_ref, first_ref, valid_ref, h_ref, w_ref, y_ref, wb_ref):
    i = pl.program_id(0)

    @pl.when(jnp.logical_and(valid_ref[i] == 1, first_ref[i] == 1))
    def _():
        wb_ref[...] = w_ref[...].astype(BF16)

    @pl.when(valid_ref[i] == 1)
    def _():
        y_ref[...] = jnp.dot(h_ref[...], wb_ref[...], preferred_element_type=F32)


def expert_down(h_rows, w_dn, blk_e, blk_first, blk_valid, blk_idx, tm):
    n_rows, ff = h_rows.shape
    d = w_dn.shape[2]
    n_blk = blk_e.shape[0]
    grid_spec = pltpu.PrefetchScalarGridSpec(
        num_scalar_prefetch=4,
        grid=(n_blk,),
        in_specs=[pl.BlockSpec((tm, ff), lambda i, e, fi, va, b: (b[i], 0)),
                  pl.BlockSpec((None, ff, d), lambda i, e, fi, va, b: (e[i], 0, 0))],
        out_specs=pl.BlockSpec((tm, d), lambda i, e, fi, va, b: (b[i], 0)),
        scratch_shapes=[pltpu.VMEM((ff, d), BF16)],
    )

    def kern(e_ref, first_ref, valid_ref, b_ref, h_ref, w_ref, y_ref, wb_ref):
        _expert_dn_kernel(e_ref, first_ref, valid_ref, h_ref, w_ref, y_ref, wb_ref)

    return pl.pallas_call(
        kern,
        out_shape=jax.ShapeDtypeStruct((n_rows, d), F32),
        grid_spec=grid_spec,
        compiler_params=_params(("arbitrary",)),
        name="expert_down",
    )(blk_e, blk_first, blk_valid, blk_idx, h_rows, w_dn)


def _combine_kernel(pos_ref, gate_ref, y_hbm, ysh_ref, x_ref, g_ref, b_ref, o_ref, obf_ref, buf_ref, sem, *, alpha):
    tc = x_ref.shape[0]
    n = tc * MOE_TOP_K

    def row_copy(a, src):
        return pltpu.make_async_copy(y_hbm.at[pl.ds(src, 1), :],
                                     buf_ref.at[a % MOE_TOP_K, pl.ds(a // MOE_TOP_K, 1), :], sem)

    def issue(a, carry):
        row_copy(a, pos_ref[0, a]).start()
        return carry

    lax.fori_loop(0, n, issue, 0)

    def drain(a, carry):
        row_copy(a, 0).wait()
        return carry

    lax.fori_loop(0, n, drain, 0)
    y = ysh_ref[...]
    gate = gate_ref[...]
    for kk in range(MOE_TOP_K):
        y = y + gate[:, kk:kk + 1] * buf_ref[kk]
    out = _layer_norm_rows(alpha * x_ref[...] + y, g_ref[...], b_ref[...])
    o_ref[...] = out
    obf_ref[...] = out.astype(BF16)


def moe_combine(pos, gate, y_rows, y_shared, x, g, b, alpha):
    t, d = x.shape
    tc = _tile(t, (32, 16))
    n_t = t // tc
    row = pl.BlockSpec((tc, d), lambda i: (i, 0))
    vec = pl.BlockSpec((1, d), lambda i: (0, 0))
    return pl.pallas_call(
        functools.partial(_combine_kernel, alpha=alpha),
        out_shape=(jax.ShapeDtypeStruct((t, d), F32), jax.ShapeDtypeStruct((t, d), BF16)),
        grid=(n_t,),
        in_specs=[pl.BlockSpec((None, 1, tc * MOE_TOP_K), lambda i: (i, 0, 0), memory_space=pltpu.SMEM),
                  pl.BlockSpec((tc, MOE_TOP_K), lambda i: (i, 0)),
                  pl.BlockSpec(memory_space=pl.ANY),
                  row, row, vec, vec],
        out_specs=(row, row),
        scratch_shapes=[pltpu.VMEM((MOE_TOP_K, tc, d), F32), pltpu.SemaphoreType.DMA(())],
        compiler_params=_params(("arbitrary",)),
        name="moe_combine",
    )(pos.reshape(n_t, 1, tc * MOE_TOP_K), gate, y_rows, y_shared, x, g.reshape(1, d), b.reshape(1, d))


def _moe_plan(eidx, n_exp, tm, nf):
    t, k = eidx.shape
    n_assign = t * k
    n_blk = -(-n_assign // tm) + n_exp
    flat_e = eidx.reshape(n_assign)
    onehot = (flat_e[:, None] == jnp.arange(n_exp, dtype=jnp.int32)[None, :]).astype(jnp.int32)
    rank = jnp.sum(onehot * (jnp.cumsum(onehot, axis=0) - 1), axis=1)
    counts = jnp.sum(onehot, axis=0)
    nb_e = (counts + tm - 1) // tm
    blk_end = jnp.cumsum(nb_e)
    blk_start = blk_end - nb_e
    n_used = blk_end[-1]
    pos = jnp.sum(onehot * (blk_start * tm)[None, :], axis=1) + rank
    order = jnp.argsort(flat_e)
    start = jnp.cumsum(counts) - counts
    blk = jnp.arange(n_blk, dtype=jnp.int32)
    blk_c = jnp.minimum(blk, n_used - 1)
    blk_e = jnp.minimum(jnp.searchsorted(blk_end, blk_c, side='right'), n_exp - 1).astype(jnp.int32)
    blk_valid = (blk < n_used).astype(jnp.int32)
    blk_first = (blk_c == blk_start[blk_e]).astype(jnp.int32)
    row = jnp.arange(n_blk * tm, dtype=jnp.int32)
    row_e = jnp.repeat(blk_e, tm)
    off = row - (blk_start * tm)[row_e]
    live = jnp.logical_and(off < counts[row_e], jnp.repeat(blk_valid, tm) == 1)
    src = order[jnp.clip(start[row_e] + off, 0, n_assign - 1)] // k
    row_tok = jnp.where(live, src, 0).astype(jnp.int32)
    n_items = n_blk * nf
    item_end = jnp.cumsum(nb_e * nf)
    it = jnp.minimum(jnp.arange(n_items, dtype=jnp.int32), item_end[-1] - 1)
    it_e = jnp.minimum(jnp.searchsorted(item_end, it, side='right'), n_exp - 1).astype(jnp.int32)
    local = it - (item_end - nb_e * nf)[it_e]
    nbe = jnp.maximum(nb_e[it_e], 1)
    it_f = (local // nbe).astype(jnp.int32)
    it_r = local % nbe
    it_blk = (blk_start[it_e] + it_r).astype(jnp.int32)
    it_first = (it_r == 0).astype(jnp.int32)
    it_valid = (jnp.arange(n_items) < item_end[-1]).astype(jnp.int32)
    return dict(pos=pos.reshape(t, k).astype(jnp.int32), row_tok=row_tok, n_used=n_used.astype(jnp.int32),
                blk_e=blk_e, blk_first=blk_first, blk_valid=blk_valid, blk_idx=blk_c.astype(jnp.int32),
                it_e=it_e, it_f=it_f, it_blk=it_blk, it_first=it_first, it_valid=it_valid)


def _ff_tile(ff):
    return _tile(ff, (384, 256, 128))


def moe_layer(x, x_bf, w_router, b_router, w_gu, w_dn, ws_gu, ws_dn, ln_g, ln_b, alpha):
    t, d = x.shape
    n_exp = w_router.shape[1]
    ff = w_dn.shape[1]
    tm = _tile(t, (256, 128, 64, 32, 16))
    tf = _ff_tile(ff)
    nf = ff // tf
    eidx, gate = moe_router(x, w_router, b_router)
    plan = _moe_plan(eidx, n_exp, tm, nf)
    x_rows = gather_rows(x, plan['row_tok'], plan['n_used'], tm)
    h_rows = expert_gate_up(x_rows, w_gu, plan['it_e'], plan['it_f'], plan['it_blk'], plan['it_first'],
                            plan['it_valid'], tm, tf)
    y_rows = expert_down(h_rows, w_dn, plan['blk_e'], plan['blk_first'], plan['blk_valid'], plan['blk_idx'], tm)
    sff = ws_dn.shape[0]
    stf = _ff_tile(sff)
    snf = sff // stf
    nsb = t // tm
    zeros = lambda n: jnp.zeros((n,), jnp.int32)
    ones = lambda n: jnp.ones((n,), jnp.int32)
    s_blk = jnp.tile(jnp.arange(nsb, dtype=jnp.int32), snf)
    s_f = jnp.repeat(jnp.arange(snf, dtype=jnp.int32), nsb)
    s_first = (s_blk == 0).astype(jnp.int32)
    hs = expert_gate_up(x_bf, ws_gu[None], zeros(nsb * snf), s_f, s_blk, s_first, ones(nsb * snf), tm, stf)
    sb = jnp.arange(nsb, dtype=jnp.int32)
    ys = expert_down(hs, ws_dn[None], zeros(nsb), (sb == 0).astype(jnp.int32), ones(nsb), sb, tm)
    return moe_combine(plan['pos'], gate, y_rows, ys, x, ln_g, ln_b, alpha)


def kernel(x_prompt, x_sample, state_hgrn, cache_swa_k, cache_swa_v, state_conv, hg_w_in, hg_lb, hg_norm, hg_w_o, swa_w_qkv, swa_b_qkv, swa_sinks, swa_w_o, swa_b_o, cv_w_pw1, cv_b_pw1, cv_w_dw, cv_b_dw, cv_ln_g, cv_ln_b, cv_w_pw2, cv_b_pw2, ln_mix_g, ln_mix_b, ln_ffn_g, ln_ffn_b, moe_w_router, moe_b_router, moe_w_gu, moe_w_dn, moe_ws_gu, moe_ws_dn):
    bp, lp, d = x_prompt.shape
    bs, ls, _ = x_sample.shape
    tp, ts = bp * lp, bs * ls
    depth = ln_mix_g.shape[0]
    alpha = (2 * depth) ** 0.25
    nh = d // HG_EXPAND
    kvd = SWA_KV_HEADS * SWA_HEAD_DIM
    buf_len = cache_swa_k.shape[2]
    cw = cv_w_dw.shape[1]

    lb_p = jax.nn.softmax(hg_lb.astype(F32), axis=0)
    lb_all = jnp.concatenate([jnp.zeros_like(lb_p[:1]), jnp.cumsum(lb_p[1:], axis=0)], axis=0)

    x = jnp.concatenate([x_prompt.reshape(tp, d), x_sample.reshape(ts, d)], axis=0)
    x_bf = x.astype(BF16)
    hg_p, hg_s, k_p, k_s, v_p, v_s, cv_p, cv_s = [], [], [], [], [], [], [], []
    for layer in range(depth):
        kind, j = layer % N_MIXERS, layer // N_MIXERS
        if kind == 0:
            hcat = matmul(x_bf, hg_w_in[j])
            o, sp = hgrn_recurrence(hcat, lb_all[j], hg_norm[j], None, None,
                                    row0=0, nb=bp, seq=lp, chunk=min(lp, HG_CHUNK))
            o, ss = hgrn_recurrence(hcat, lb_all[j], hg_norm[j], state_hgrn[j], o,
                                    row0=tp, nb=bs, seq=ls, chunk=min(ls, HG_CHUNK))
            mix = matmul(o, hg_w_o[j])
            hg_p.append(sp)
            hg_s.append(ss)
        elif kind == 1:
            qkv = matmul(x_bf, swa_w_qkv[j], swa_b_qkv[j])
            o = swa_attention(qkv, None, None, swa_sinks[j], None, row0=0, nb=bp, seq=lp, d=d)
            ck = cache_swa_k[j].reshape(bs * buf_len, kvd)
            cv = cache_swa_v[j].reshape(bs * buf_len, kvd)
            o = swa_attention(qkv, ck, cv, swa_sinks[j], o, row0=tp, nb=bs, seq=ls, d=d)
            mix = matmul(o, swa_w_o[j], swa_b_o[j])
            k_new = qkv[:, d:d + kvd]
            v_new = qkv[:, d + kvd:]
            tail = lambda a: a[:tp].reshape(bp, lp, SWA_KV_HEADS, SWA_HEAD_DIM)[:, lp - buf_len:]
            k_p.append(tail(k_new))
            v_p.append(tail(v_new))
            new = lambda a: a[tp:].reshape(bs, ls, SWA_KV_HEADS, SWA_HEAD_DIM)
            k_s.append(jnp.concatenate([cache_swa_k[j], new(k_new)], axis=1)[:, ls:])
            v_s.append(jnp.concatenate([cache_swa_v[j], new(v_new)], axis=1)[:, ls:])
        else:
            hcat = matmul(x_bf, cv_w_pw1[j], cv_b_pw1[j])
            cargs = (cv_w_dw[j], cv_b_dw[j], cv_ln_g[j], cv_ln_b[j])
            y, u = conv_block(hcat, None, *cargs, None, None, row0=0, nb=bp, seq=lp)
            keep = cw - 1
            hist = jnp.pad(state_conv[j], ((0, 0), (CONV_HALO - keep, 0), (0, 0)))
            y, u = conv_block(hcat, hist, *cargs, y, u, row0=tp, nb=bs, seq=ls)
            mix = matmul(y, cv_w_pw2[j], cv_b_pw2[j])
            cv_p.append(u[:tp].reshape(bp, lp, d)[:, lp - keep:])
            cv_s.append(jnp.concatenate([state_conv[j], u[tp:].reshape(bs, ls, d)], axis=1)[:, ls:])
        x, x_bf = ln_residual(x, mix, ln_mix_g[layer], ln_mix_b[layer], alpha)
        x, x_bf = moe_layer(x, x_bf, moe_w_router[layer], moe_b_router[layer], moe_w_gu[layer], moe_w_dn[layer],
                            moe_ws_gu[layer], moe_ws_dn[layer], ln_ffn_g[layer], ln_ffn_b[layer], alpha)
    return (x[:tp].reshape(bp, lp, d), x[tp:].reshape(bs, ls, d),
            jnp.stack(hg_p), jnp.stack(hg_s), jnp.stack(k_p), jnp.stack(k_s), jnp.stack(v_p), jnp.stack(v_s),
            jnp.stack(cv_p), jnp.stack(cv_s))
```

```python
import functools
import math

import jax
import jax.numpy as jnp
from jax import lax
from jax.experimental import pallas as pl
from jax.experimental.pallas import tpu as pltpu

F32 = jnp.float32
BF16 = jnp.bfloat16

N_MIXERS = 3
HG_EXPAND = 128
HG_CHUNK = 32
SWA_WINDOW = 128
SWA_HEAD_DIM = 64
SWA_KV_HEADS = 8
MOE_TOP_K = 8
MOE_GROUPS = 8
MOE_TOPK_GROUPS = 4
MOE_ROUTED_SCALE = 2.5
LN_EPS = 1e-5
RMS_EPS = 1e-6

LANES = 128
SUBLANES = 8
VMEM_LIMIT_BYTES = 56 * 1024 * 1024


def _tile(dim, prefs):
    for p in prefs:
        if dim % p == 0:
            return p
    return dim


def _params(sem):
    return pltpu.CompilerParams(dimension_semantics=sem, vmem_limit_bytes=VMEM_LIMIT_BYTES)


def _sigmoid(x):
    return 1.0 / (1.0 + jnp.exp(-x))


def _lookup(table, idx):
    n = table.shape[0]
    hit = idx[..., None] == jnp.arange(n, dtype=jnp.int32)
    return jnp.sum(jnp.where(hit, table, 0), axis=-1)


def _mm_kernel(*refs, has_bias):
    if has_bias:
        x_ref, w_ref, b_ref, o_ref, wbf_ref = refs
    else:
        x_ref, w_ref, o_ref, wbf_ref = refs

    @pl.when(pl.program_id(1) == 0)
    def _():
        wbf_ref[...] = w_ref[...].astype(BF16)

    acc = jnp.dot(x_ref[...].astype(BF16), wbf_ref[...], preferred_element_type=F32)
    if has_bias:
        acc = acc + b_ref[...]
    o_ref[...] = acc.astype(o_ref.dtype)


def matmul(x, w, layer, b=None, out_dtype=F32):
    m, k = x.shape
    n = w.shape[2]
    tm = _tile(m, (768, 512, 256, 128, 64, 32, 16, 8))
    tn = _tile(n, (512, 256, 128))
    in_specs = [pl.BlockSpec((tm, k), lambda j, i: (i, 0)),
                pl.BlockSpec((None, k, tn), lambda j, i: (layer, 0, j))]
    args = [x, w]
    if b is not None:
        in_specs.append(pl.BlockSpec((None, 1, tn), lambda j, i: (layer, 0, j)))
        args.append(b.reshape(b.shape[0], 1, n))
    return pl.pallas_call(
        functools.partial(_mm_kernel, has_bias=b is not None),
        out_shape=jax.ShapeDtypeStruct((m, n), out_dtype),
        grid=(n // tn, m // tm),
        in_specs=in_specs,
        out_specs=pl.BlockSpec((tm, tn), lambda j, i: (i, j)),
        scratch_shapes=[pltpu.VMEM((k, tn), BF16)],
        compiler_params=_params(("arbitrary", "arbitrary")),
        name="matmul",
    )(*args)


def _layer_norm_rows(z, g, b):
    mu = jnp.mean(z, axis=-1, keepdims=True)
    zc = z - mu
    var = jnp.mean(zc * zc, axis=-1, keepdims=True)
    return zc * lax.rsqrt(var + LN_EPS) * g + b


def _ln_res_kernel(x_ref, m_ref, g_ref, b_ref, o_ref, obf_ref, *, alpha):
    y = _layer_norm_rows(alpha * x_ref[...] + m_ref[...], g_ref[...], b_ref[...])
    o_ref[...] = y
    obf_ref[...] = y.astype(BF16)


def _vec_spec(d, layer, ngrid):
    if ngrid == 1:
        return pl.BlockSpec((None, 1, d), lambda i: (layer, 0, 0))
    return pl.BlockSpec((None, 1, d), lambda i, j: (layer, 0, 0))


def ln_residual(x, m, g, b, layer, alpha):
    t, d = x.shape
    tm = _tile(t, (256, 128, 64, 32, 16))
    row = pl.BlockSpec((tm, d), lambda i: (i, 0))
    vec = _vec_spec(d, layer, 1)
    return pl.pallas_call(
        functools.partial(_ln_res_kernel, alpha=alpha),
        out_shape=(jax.ShapeDtypeStruct((t, d), F32), jax.ShapeDtypeStruct((t, d), BF16)),
        grid=(t // tm,),
        in_specs=[row, row, vec, vec],
        out_specs=(row, row),
        compiler_params=_params(("arbitrary",)),
        name="ln_residual",
    )(x, m, g.reshape(g.shape[0], 1, d), b.reshape(b.shape[0], 1, d))


def _hgrn_kernel(*refs, chunk, has_s0):
    if has_s0:
        q_ref, f_ref, i_ref, g_ref, lb_ref, gn_ref, s0_ref, o_ref, sout_ref, st_ref = refs
    else:
        q_ref, f_ref, i_ref, g_ref, lb_ref, gn_ref, o_ref, sout_ref, st_ref = refs
    c_id = pl.program_id(2)

    @pl.when(c_id == 0)
    def _():
        if has_s0:
            st_ref[...] = s0_ref[...].T
        else:
            st_ref[...] = jnp.zeros_like(st_ref)

    rb, dk = q_ref.shape
    nc = rb // chunk
    n_sub = max(chunk // SUBLANES, 1)
    sub = chunk // n_sub
    lb = lb_ref[...]
    log_lb = jnp.log(lb)
    log_1m_lb = jnp.log1p(-lb)
    hq = q_ref[...]
    hf = f_ref[...]
    v = i_ref[...]
    hg = g_ref[...]
    q = hq * _sigmoid(hq)
    log_sig = jnp.minimum(hf, 0.0) - jnp.log1p(jnp.exp(-jnp.abs(hf)))
    bt = log_1m_lb + log_sig
    log_f = jnp.maximum(log_lb, bt) + jnp.log1p(jnp.exp(-jnp.abs(log_lb - bt)))
    k = (1.0 - lb) * _sigmoid(-hf)

    tril = (lax.broadcasted_iota(jnp.int32, (chunk, chunk), 0)
            >= lax.broadcasted_iota(jnp.int32, (chunk, chunk), 1)).astype(F32)
    if nc > 1:
        logf_w = jnp.concatenate([log_f[c * chunk:(c + 1) * chunk, :] for c in range(nc)], axis=1)
    else:
        logf_w = log_f
    cum_w = jnp.dot(tril, logf_w, precision=lax.Precision.HIGHEST, preferred_element_type=F32)
    if nc > 1:
        cum = jnp.concatenate([cum_w[:, c * dk:(c + 1) * dk] for c in range(nc)], axis=0)
    else:
        cum = cum_w
    cum3 = cum.reshape(nc, chunk, dk)
    q3 = q.reshape(nc, chunk, dk)
    k3 = k.reshape(nc, chunk, dk)

    lane_t = lax.broadcasted_iota(jnp.int32, (nc, sub, chunk), 2)
    s_in_sub = lax.broadcasted_iota(jnp.int32, (nc, sub, dk), 1)
    att_t = [jnp.zeros((nc, sub, chunk), F32) for _ in range(n_sub)]
    for t in range(chunk):
        jt = t // sub
        qt = q3[:, t:t + 1, :]
        ct = cum3[:, t:t + 1, :]
        for j in range(jt + 1):
            rel = ct - cum3[:, j * sub:(j + 1) * sub, :]
            if j == jt:
                rel = jnp.where(s_in_sub <= (t - jt * sub), rel, -jnp.inf)
            p = (qt * jnp.exp(rel)) * k3[:, j * sub:(j + 1) * sub, :]
            r = jnp.sum(p, axis=-1, keepdims=True)
            att_t[j] = jnp.where(lane_t == t, r, att_t[j])

    last3 = cum3[:, chunk - 1:chunk, :]
    qe = (q * jnp.exp(cum)).astype(BF16)
    kdec = (k3 * jnp.exp(last3 - cum3)).reshape(rb, dk).astype(BF16)
    vb = v.astype(BF16)
    tn_dims = (((0,), (0,)), ((), ()))
    nt_dims = (((1,), (1,)), ((), ()))
    st = st_ref[...]
    o_parts = []
    for c in range(nc):
        rows = slice(c * chunk, (c + 1) * chunk)
        att_c = att_t[0][c] if n_sub == 1 else jnp.concatenate([a[c] for a in att_t], axis=0)
        o_c = lax.dot_general(qe[rows], st.astype(BF16), nt_dims, preferred_element_type=F32)
        o_c = o_c + lax.dot_general(att_c.astype(BF16), vb[rows], tn_dims, preferred_element_type=F32)
        inc = lax.dot_general(vb[rows], kdec[rows], tn_dims, preferred_element_type=F32)
        st = jnp.exp(cum[(c + 1) * chunk - 1:(c + 1) * chunk, :]) * st + inc
        o_parts.append(o_c)
    st_ref[...] = st
    o = o_parts[0] if nc == 1 else jnp.concatenate(o_parts, axis=0)
    o = o * lax.rsqrt(jnp.mean(o * o, axis=-1, keepdims=True) + RMS_EPS) * gn_ref[...] * (hg * _sigmoid(hg))
    o_ref[...] = o.astype(o_ref.dtype)

    @pl.when(c_id == pl.num_programs(2) - 1)
    def _():
        sout_ref[...] = st.T


def hgrn_recurrence(hcat, lb, gnorm, s0, o_prev, *, row0, nb, seq, chunk):
    t, d4 = hcat.shape
    d = d4 // 4
    hd = HG_EXPAND
    nh = d // hd
    rb = max(_tile(seq, (512, 256, 128, 64, 32, 16, 8)), chunk)
    nrb = seq // rb
    rblk0 = row0 // rb

    def col(off):
        return pl.BlockSpec((rb, hd), lambda b, h, c, off=off: (rblk0 + b * nrb + c, off + h))

    in_specs = [col(0), col(nh), col(2 * nh), col(3 * nh),
                pl.BlockSpec((1, hd), lambda b, h, c: (0, h)),
                pl.BlockSpec((1, hd), lambda b, h, c: (0, 0))]
    args = [hcat, hcat, hcat, hcat, lb.reshape(1, d), gnorm.reshape(1, hd)]
    if s0 is not None:
        in_specs.append(pl.BlockSpec((None, None, hd, hd), lambda b, h, c: (b, h, 0, 0)))
        args.append(s0)
    aliases = {}
    if o_prev is not None:
        in_specs.append(pl.BlockSpec(memory_space=pl.ANY))
        args.append(o_prev)
        aliases = {len(args) - 1: 0}

    def kern(*refs):
        if o_prev is not None:
            n_in = len(args)
            refs = refs[:n_in - 1] + refs[n_in:]
        _hgrn_kernel(*refs, chunk=chunk, has_s0=s0 is not None)

    return pl.pallas_call(
        kern,
        out_shape=(jax.ShapeDtypeStruct((t, d), F32), jax.ShapeDtypeStruct((nb, nh, hd, hd), F32)),
        grid=(nb, nh, nrb),
        in_specs=in_specs,
        out_specs=(pl.BlockSpec((rb, hd), lambda b, h, c: (rblk0 + b * nrb + c, h)),
                   pl.BlockSpec((None, None, hd, hd), lambda b, h, c: (b, h, 0, 0))),
        scratch_shapes=[pltpu.VMEM((hd, hd), F32)],
        input_output_aliases=aliases,
        compiler_params=_params(("arbitrary", "arbitrary", "arbitrary")),
        name="hgrn_recurrence",
    )(*args)


def _swa_kernel(q_ref, kp_ref, ko_ref, vp_ref, vo_ref, sink_ref, o_ref, *, group, first_block_has_no_prev):
    lq = q_ref.shape[0]
    sp_len = kp_ref.shape[0]
    so_len = ko_ref.shape[0]
    hd = SWA_HEAD_DIM
    scale = hd ** -0.5
    i_p = lax.broadcasted_iota(jnp.int32, (lq, sp_len), 0)
    j_p = lax.broadcasted_iota(jnp.int32, (lq, sp_len), 1)
    mask_p = j_p > i_p
    if first_block_has_no_prev:
        mask_p = jnp.logical_and(mask_p, pl.program_id(1) > 0)
    i_o = lax.broadcasted_iota(jnp.int32, (lq, so_len), 0)
    j_o = lax.broadcasted_iota(jnp.int32, (lq, so_len), 1)
    mask_o = j_o <= i_o
    nt = (((1,), (1,)), ((), ()))
    kv_per_block = kp_ref.shape[1] // hd
    heads = [(kv, kv * group + g) for kv in range(kv_per_block) for g in range(group)]
    kvs = []
    for kv in range(kv_per_block):
        cs = slice(kv * hd, (kv + 1) * hd)
        kvs.append((kp_ref[:, cs].astype(BF16), ko_ref[:, cs].astype(BF16),
                    vp_ref[:, cs].astype(BF16), vo_ref[:, cs].astype(BF16)))
    scores = []
    for kv, h in heads:
        qh = q_ref[:, h * hd:(h + 1) * hd].astype(BF16)
        s_p = lax.dot_general(qh, kvs[kv][0], nt, preferred_element_type=F32) * scale
        s_o = lax.dot_general(qh, kvs[kv][1], nt, preferred_element_type=F32) * scale
        scores.append((jnp.where(mask_p, s_p, -jnp.inf), jnp.where(mask_o, s_o, -jnp.inf)))
    probs = []
    for (kv, h), (s_p, s_o) in zip(heads, scores):
        sink = sink_ref[0:1, h:h + 1]
        m = jnp.maximum(jnp.maximum(jnp.max(s_p, axis=-1, keepdims=True),
                                    jnp.max(s_o, axis=-1, keepdims=True)), sink)
        p_p = jnp.exp(s_p - m)
        p_o = jnp.exp(s_o - m)
        den = (jnp.sum(p_p, axis=-1, keepdims=True) + jnp.sum(p_o, axis=-1, keepdims=True)
               + jnp.exp(sink - m))
        probs.append(((p_p / den).astype(BF16), (p_o / den).astype(BF16)))
    outs = []
    for (kv, h), (p_p, p_o) in zip(heads, probs):
        outs.append(jnp.dot(p_p, kvs[kv][2], preferred_element_type=F32)
                    + jnp.dot(p_o, kvs[kv][3], preferred_element_type=F32))
    for (kv, h), o in zip(heads, outs):
        o_ref[:, h * hd:(h + 1) * hd] = o.astype(o_ref.dtype)


def swa_attention(qkv, k_prev, v_prev, sinks, o_prev, *, row0, nb, seq, d):
    t = qkv.shape[0]
    hd = SWA_HEAD_DIM
    group = d // hd // SWA_KV_HEADS
    kvb = LANES // hd
    n_kvb = SWA_KV_HEADS // kvb
    qw = kvb * group * hd
    lq = min(seq, SWA_WINDOW)
    nq = seq // lq
    rblk0 = row0 // lq
    kcol0 = d // LANES
    vcol0 = (d + SWA_KV_HEADS * hd) // LANES
    is_prompt = k_prev is None
    own = lambda c0: pl.BlockSpec((lq, LANES), lambda b, n, p: (rblk0 + b * nq + n, c0 + p))
    if is_prompt:
        prev = lambda c0: pl.BlockSpec(
            (lq, LANES), lambda b, n, p: (rblk0 + b * nq + jnp.maximum(n - 1, 0), c0 + p))
        kp_spec, vp_spec = prev(kcol0), prev(vcol0)
        k_prev = v_prev = qkv
    else:
        kp_spec = vp_spec = pl.BlockSpec((SWA_WINDOW, LANES), lambda b, n, p: (b, p))
    in_specs = [pl.BlockSpec((lq, qw), lambda b, n, p: (rblk0 + b * nq + n, p)),
                kp_spec, own(kcol0), vp_spec, own(vcol0),
                pl.BlockSpec((None, 1, kvb * group), lambda b, n, p: (p, 0, 0))]
    args = [qkv, k_prev, qkv, v_prev, qkv, sinks.reshape(n_kvb, 1, kvb * group)]
    aliases = {}
    if o_prev is not None:
        in_specs.append(pl.BlockSpec(memory_space=pl.ANY))
        args.append(o_prev)
        aliases = {len(args) - 1: 0}

    def kern(*refs):
        if o_prev is not None:
            n_in = len(args)
            refs = refs[:n_in - 1] + refs[n_in:]
        _swa_kernel(*refs, group=group, first_block_has_no_prev=is_prompt)

    return pl.pallas_call(
        kern,
        out_shape=jax.ShapeDtypeStruct((t, d), F32),
        grid=(nb, nq, n_kvb),
        in_specs=in_specs,
        out_specs=pl.BlockSpec((lq, qw), lambda b, n, p: (rblk0 + b * nq + n, p)),
        input_output_aliases=aliases,
        compiler_params=_params(("arbitrary", "arbitrary", "arbitrary")),
        name="swa_attention",
    )(*args)


CONV_HALO = 32


def _conv_kernel(*refs, width, raw_halo):
    if raw_halo:
        a_ref, gt_ref, ha_ref, hgt_ref, wdw_ref, bdw_ref, lng_ref, lnb_ref, y_ref, u_ref, full_ref, acc_ref = refs
    else:
        a_ref, gt_ref, hu_ref, wdw_ref, bdw_ref, lng_ref, lnb_ref, y_ref, u_ref, full_ref, acc_ref = refs
    rb, c = a_ref.shape
    u = a_ref[...] * _sigmoid(gt_ref[...])
    u_ref[...] = u
    if raw_halo:
        hu = ha_ref[...] * _sigmoid(hgt_ref[...])
        hu = jnp.where(pl.program_id(1) > 0, hu, 0.0)
    else:
        hu = hu_ref[...]
    full_ref[0:CONV_HALO, :] = hu
    full_ref[CONV_HALO:CONV_HALO + rb, :] = u
    lead = CONV_HALO - (width - 1)
    rt = min(rb, 64)

    def col_tile(ct, carry):
        c0 = pl.multiple_of(ct * LANES, LANES)
        for r0 in range(0, rb, rt):
            acc = jnp.broadcast_to(bdw_ref[:, pl.ds(c0, LANES)], (rt, LANES))
            for w in range(width):
                acc = acc + full_ref[r0 + lead + w:r0 + lead + w + rt, pl.ds(c0, LANES)] * wdw_ref[w:w + 1, pl.ds(c0, LANES)]
            acc_ref[r0:r0 + rt, pl.ds(c0, LANES)] = acc
        return carry

    lax.fori_loop(0, c // LANES, col_tile, 0)
    y = _layer_norm_rows(acc_ref[...], lng_ref[...], lnb_ref[...])
    y_ref[...] = (y * _sigmoid(y)).astype(y_ref.dtype)


def conv_block(hcat, hist, w_dw, b_dw, ln_g, ln_b, layer, y_prev, u_prev, *, row0, nb, seq):
    t, c2 = hcat.shape
    c = c2 // 2
    nl, width, _ = w_dw.shape
    rb = _tile(seq, (256, 128, 64, 32, 16, 8))
    nrb = seq // rb
    rblk0 = row0 // rb
    raw_halo = hist is None
    main = lambda half: pl.BlockSpec((rb, c), lambda b, n: (rblk0 + b * nrb + n, half))
    vec = _vec_spec(c, layer, 2)
    in_specs = [main(0), main(1)]
    args = [hcat, hcat]
    if raw_halo:
        hpb = rb // CONV_HALO
        halo = lambda half: pl.BlockSpec(
            (CONV_HALO, c), lambda b, n: (jnp.maximum((rblk0 + b * nrb + n) * hpb - 1, 0), half))
        in_specs += [halo(0), halo(1)]
        args += [hcat, hcat]
    else:
        in_specs.append(pl.BlockSpec((None, CONV_HALO, c), lambda b, n: (b, 0, 0)))
        args.append(hist)
    in_specs += [pl.BlockSpec((None, width, c), lambda b, n: (layer, 0, 0)), vec, vec, vec]
    args += [w_dw, b_dw.reshape(nl, 1, c), ln_g.reshape(nl, 1, c), ln_b.reshape(nl, 1, c)]
    n_real = len(args)
    aliases = {}
    if y_prev is not None:
        in_specs += [pl.BlockSpec(memory_space=pl.ANY), pl.BlockSpec(memory_space=pl.ANY)]
        args += [y_prev, u_prev]
        aliases = {n_real: 0, n_real + 1: 1}

    def kern(*refs):
        if y_prev is not None:
            refs = refs[:n_real] + refs[n_real + 2:]
        _conv_kernel(*refs, width=width, raw_halo=raw_halo)

    out_blk = pl.BlockSpec((rb, c), lambda b, n: (rblk0 + b * nrb + n, 0))
    return pl.pallas_call(
        kern,
        out_shape=(jax.ShapeDtypeStruct((t, c), F32), jax.ShapeDtypeStruct((t, c), F32)),
        grid=(nb, nrb),
        in_specs=in_specs,
        out_specs=(out_blk, out_blk),
        scratch_shapes=[pltpu.VMEM((CONV_HALO + rb, c), F32), pltpu.VMEM((rb, c), F32)],
        input_output_aliases=aliases,
        compiler_params=_params(("arbitrary", "arbitrary")),
        name="conv_block",
    )(*args)


def _router_kernel(x_ref, w_ref, b_ref, eidx_ref, gate_ref, rank_ref, cnt_ref, run_ref):
    @pl.when(pl.program_id(0) == 0)
    def _():
        run_ref[...] = jnp.zeros_like(run_ref)

    logits = jnp.dot(x_ref[...], w_ref[...], precision=lax.Precision.HIGHEST, preferred_element_type=F32)
    scores = _sigmoid(logits)
    sel = scores + b_ref[...]
    tm, ne = sel.shape
    gsz = ne // MOE_GROUPS
    lane = lax.broadcasted_iota(jnp.int32, (tm, ne), 1).astype(F32)
    grp = lax.broadcasted_iota(jnp.int32, (tm, ne), 1) // gsz
    grp_f = grp.astype(F32)
    neg = -jnp.inf

    def first_argmax(v, idx_f):
        m = jnp.max(v, axis=-1, keepdims=True)
        return m, jnp.min(jnp.where(v == m, idx_f, float(ne)), axis=-1, keepdims=True)

    gscore = jnp.zeros((tm, ne), F32)
    for gi in range(MOE_GROUPS):
        in_g = grp == gi
        v = jnp.where(in_g, sel, neg)
        m1, i1 = first_argmax(v, lane)
        m2 = jnp.max(jnp.where(lane == i1, neg, v), axis=-1, keepdims=True)
        gscore = jnp.where(in_g, m1 + m2, gscore)
    gmask = jnp.zeros((tm, ne), jnp.bool_)
    for _ in range(MOE_TOPK_GROUPS):
        _, gi = first_argmax(gscore, grp_f)
        hit = grp_f == gi
        gmask = jnp.logical_or(gmask, hit)
        gscore = jnp.where(hit, neg, gscore)
    cand = jnp.where(gmask, sel, neg)
    k_iota = lax.broadcasted_iota(jnp.int32, (tm, MOE_TOP_K), 1)
    eidx = jnp.zeros((tm, MOE_TOP_K), F32)
    gate = jnp.zeros((tm, MOE_TOP_K), F32)
    chosen = jnp.zeros((tm, ne), F32)
    picks = []
    for kk in range(MOE_TOP_K):
        _, ei = first_argmax(cand, lane)
        hit = lane == ei
        wk = jnp.sum(jnp.where(hit, scores, 0.0), axis=-1, keepdims=True)
        cand = jnp.where(hit, neg, cand)
        chosen = jnp.where(hit, 1.0, chosen)
        eidx = jnp.where(k_iota == kk, ei, eidx)
        gate = jnp.where(k_iota == kk, wk, gate)
        picks.append(hit)
    gate = gate / jnp.sum(gate, axis=-1, keepdims=True) * MOE_ROUTED_SCALE
    eidx_ref[...] = eidx.astype(jnp.int32)
    gate_ref[...] = gate
    earlier = (lax.broadcasted_iota(jnp.int32, (tm, tm), 0)
               > lax.broadcasted_iota(jnp.int32, (tm, tm), 1)).astype(BF16)
    before = jnp.dot(earlier, chosen.astype(BF16), preferred_element_type=F32) + run_ref[...]
    rank = jnp.zeros((tm, MOE_TOP_K), F32)
    for kk in range(MOE_TOP_K):
        rk = jnp.sum(jnp.where(picks[kk], before, 0.0), axis=-1, keepdims=True)
        rank = jnp.where(k_iota == kk, rk, rank)
    rank_ref[...] = rank.astype(jnp.int32)
    run = run_ref[...] + jnp.sum(chosen, axis=0, keepdims=True)
    run_ref[...] = run
    cnt_ref[...] = run.astype(jnp.int32)


def moe_router(x, w_router, b_router, layer):
    t, d = x.shape
    nl, _, ne = w_router.shape
    tm = _tile(t, (256, 128, 64, 32, 16, 8))
    tok = lambda dt: jax.ShapeDtypeStruct((t, MOE_TOP_K), dt)
    tok_spec = pl.BlockSpec((tm, MOE_TOP_K), lambda i: (i, 0))
    return pl.pallas_call(
        _router_kernel,
        out_shape=(tok(jnp.int32), tok(F32), tok(jnp.int32), jax.ShapeDtypeStruct((1, ne), jnp.int32)),
        grid=(t // tm,),
        in_specs=[pl.BlockSpec((tm, d), lambda i: (i, 0)),
                  pl.BlockSpec((None, d, ne), lambda i: (layer, 0, 0)),
                  pl.BlockSpec((None, 1, ne), lambda i: (layer, 0, 0))],
        out_specs=(tok_spec, tok_spec, tok_spec, pl.BlockSpec((1, ne), lambda i: (0, 0))),
        scratch_shapes=[pltpu.VMEM((1, ne), F32)],
        compiler_params=_params(("arbitrary",)),
        name="moe_router",
    )(x, w_router, b_router.reshape(nl, 1, ne))


def _pad_pieces(tm, zr):
    pieces = [zr] * ((tm - 1) // zr)
    s = zr // 2
    while s >= SUBLANES:
        pieces.append(s)
        s //= 2
    return pieces


def _dispatch_kernel(padlo_ref, padn_ref, pos_ref, x_hbm, o_hbm, zero_ref, sem, zsem, *, tm):
    n = pos_ref.shape[-1]
    tt = n // MOE_TOP_K
    t0 = pl.program_id(0) * tt

    def row_copy(a, dst):
        return pltpu.make_async_copy(x_hbm.at[pl.ds(t0 + a // MOE_TOP_K, 1), :], o_hbm.at[pl.ds(dst, 1), :], sem)

    def issue(a, carry):
        row_copy(a, pos_ref[0, a]).start()
        return carry

    lax.fori_loop(0, n, issue, 0)

    @pl.when(pl.program_id(0) == 0)
    def _():
        zero_ref[...] = jnp.zeros_like(zero_ref)

        def fill(e, wait):
            def piece(take, off, s):
                cp = pltpu.make_async_copy(zero_ref.at[pl.ds(0, s), :], o_hbm.at[pl.ds(off, s), :], zsem)

                @pl.when(take)
                def _():
                    if wait:
                        cp.wait()
                    else:
                        cp.start()

            off = padlo_ref[e]
            left = padn_ref[e]
            head = jnp.minimum((-off) & (SUBLANES - 1), left)
            for r in range(SUBLANES - 1):
                piece(head > r, off + r, 1)
            off = off + head
            left = left - head
            for s in _pad_pieces(tm, zero_ref.shape[0]):
                take = left >= s
                piece(take, pl.multiple_of(off, SUBLANES), s)
                step = jnp.where(take, s, 0)
                off = off + step
                left = left - step

        def fill_start(e, carry):
            fill(e, False)
            return carry

        def fill_wait(e, carry):
            fill(e, True)
            return carry

        ne = padlo_ref.shape[0]
        lax.fori_loop(0, ne, fill_start, 0)
        lax.fori_loop(0, ne, fill_wait, 0)

    def drain(a, carry):
        row_copy(a, 0).wait()
        return carry

    lax.fori_loop(0, n, drain, 0)


ZERO_ROWS = 32


def dispatch_rows(x, pos, pad_lo, pad_n, n_rows, tm):
    t, d = x.shape
    tt = _tile(t, (128, 64, 32, 16, 8))
    n_t = t // tt
    grid_spec = pltpu.PrefetchScalarGridSpec(
        num_scalar_prefetch=2,
        grid=(n_t,),
        in_specs=[pl.BlockSpec((None, 1, tt * MOE_TOP_K), lambda i, lo, nn: (i, 0, 0), memory_space=pltpu.SMEM),
                  pl.BlockSpec(memory_space=pl.ANY)],
        out_specs=pl.BlockSpec(memory_space=pl.ANY),
        scratch_shapes=[pltpu.VMEM((min(ZERO_ROWS, tm), d), x.dtype),
                        pltpu.SemaphoreType.DMA(()), pltpu.SemaphoreType.DMA(())],
    )
    return pl.pallas_call(
        functools.partial(_dispatch_kernel, tm=tm),
        out_shape=jax.ShapeDtypeStruct((n_rows, d), x.dtype),
        grid_spec=grid_spec,
        compiler_params=_params(("arbitrary",)),
        name="moe_dispatch",
    )(pad_lo, pad_n, pos.reshape(n_t, 1, tt * MOE_TOP_K), x)


def _expert_gu_kernel(e_ref, f_ref, blk_ref, first_ref, valid_ref, x_ref, wg_ref, wu_ref, h_ref, wcat_ref):
    i = pl.program_id(0)
    tf = wg_ref.shape[1]

    @pl.when(jnp.logical_and(valid_ref[i] == 1, first_ref[i] == 1))
    def _():
        wcat_ref[:, 0:tf] = wg_ref[...].astype(BF16)
        wcat_ref[:, tf:2 * tf] = wu_ref[...].astype(BF16)

    @pl.when(valid_ref[i] == 1)
    def _():
        gu = jnp.dot(x_ref[...].astype(BF16), wcat_ref[...], preferred_element_type=F32)
        g = gu[:, 0:tf]
        u = gu[:, tf:2 * tf]
        h_ref[...] = (g * _sigmoid(g) * u).astype(h_ref.dtype)


def expert_gate_up(x_rows, w_gu, layer, item_e, item_f, item_blk, item_first, item_valid, tm, tf):
    n_rows, d = x_rows.shape
    ff = w_gu.shape[3] // 2
    n_items = item_e.shape[0]
    nf = ff // tf
    grid_spec = pltpu.PrefetchScalarGridSpec(
        num_scalar_prefetch=5,
        grid=(n_items,),
        in_specs=[pl.BlockSpec((tm, d), lambda i, e, f, b, fi, va: (b[i], 0)),
                  pl.BlockSpec((None, None, d, tf), lambda i, e, f, b, fi, va: (layer, e[i], 0, f[i])),
                  pl.BlockSpec((None, None, d, tf), lambda i, e, f, b, fi, va: (layer, e[i], 0, nf + f[i]))],
        out_specs=pl.BlockSpec((tm, tf), lambda i, e, f, b, fi, va: (b[i], f[i])),
        scratch_shapes=[pltpu.VMEM((d, 2 * tf), BF16)],
    )
    return pl.pallas_call(
        _expert_gu_kernel,
        out_shape=jax.ShapeDtypeStruct((n_rows, ff), BF16),
        grid_spec=grid_spec,
        compiler_params=_params(("arbitrary",)),
        name="expert_gate_up",
    )(item_e, item_f, item_blk, item_first, item_valid, x_rows, w_gu, w_gu)


def _expert_dn_kernel(e_ref, first_ref, valid_ref, blk_ref, h_ref, w_ref, y_ref, wb_ref):
    i = pl.program_id(0)

    @pl.when(jnp.logical_and(valid_ref[i] == 1, first_ref[i] == 1))
    def _():
        wb_ref[...] = w_ref[...].astype(BF16)

    @pl.when(valid_ref[i] == 1)
    def _():
        y_ref[...] = jnp.dot(h_ref[...], wb_ref[...], preferred_element_type=F32)


def expert_down(h_rows, w_dn, layer, blk_e, blk_first, blk_valid, blk_idx, tm):
    n_rows, ff = h_rows.shape
    d = w_dn.shape[3]
    n_blk = blk_e.shape[0]
    grid_spec = pltpu.PrefetchScalarGridSpec(
        num_scalar_prefetch=4,
        grid=(n_blk,),
        in_specs=[pl.BlockSpec((tm, ff), lambda i, e, fi, va, b: (b[i], 0)),
                  pl.BlockSpec((None, None, ff, d), lambda i, e, fi, va, b: (layer, e[i], 0, 0))],
        out_specs=pl.BlockSpec((tm, d), lambda i, e, fi, va, b: (b[i], 0)),
        scratch_shapes=[pltpu.VMEM((ff, d), BF16)],
    )
    return pl.pallas_call(
        _expert_dn_kernel,
        out_shape=jax.ShapeDtypeStruct((n_rows, d), F32),
        grid_spec=grid_spec,
        compiler_params=_params(("arbitrary",)),
        name="expert_down",
    )(blk_e, blk_first, blk_valid, blk_idx, h_rows, w_dn)


def _combine_kernel(pos_ref, gate_ref, y_hbm, ysh_ref, x_ref, g_ref, b_ref, o_ref, obf_ref, buf_ref, sem, *, alpha):
    tc = x_ref.shape[0]
    n = tc * MOE_TOP_K

    def row_copy(a, src):
        return pltpu.make_async_copy(y_hbm.at[pl.ds(src, 1), :],
                                     buf_ref.at[a % MOE_TOP_K, pl.ds(a // MOE_TOP_K, 1), :], sem)

    def issue(a, carry):
        row_copy(a, pos_ref[0, a]).start()
        return carry

    lax.fori_loop(0, n, issue, 0)

    def drain(a, carry):
        row_copy(a, 0).wait()
        return carry

    lax.fori_loop(0, n, drain, 0)
    y = ysh_ref[...]
    gate = gate_ref[...]
    for kk in range(MOE_TOP_K):
        y = y + gate[:, kk:kk + 1] * buf_ref[kk]
    out = _layer_norm_rows(alpha * x_ref[...] + y, g_ref[...], b_ref[...])
    o_ref[...] = out
    obf_ref[...] = out.astype(BF16)


def moe_combine(pos, gate, y_rows, y_shared, x, g, b, layer, alpha):
    t, d = x.shape
    tc = _tile(t, (32, 16))
    n_t = t // tc
    row = pl.BlockSpec((tc, d), lambda i: (i, 0))
    vec = _vec_spec(d, layer, 1)
    return pl.pallas_call(
        functools.partial(_combine_kernel, alpha=alpha),
        out_shape=(jax.ShapeDtypeStruct((t, d), F32), jax.ShapeDtypeStruct((t, d), BF16)),
        grid=(n_t,),
        in_specs=[pl.BlockSpec((None, 1, tc * MOE_TOP_K), lambda i: (i, 0, 0), memory_space=pltpu.SMEM),
                  pl.BlockSpec((tc, MOE_TOP_K), lambda i: (i, 0)),
                  pl.BlockSpec(memory_space=pl.ANY),
                  row, row, vec, vec],
        out_specs=(row, row),
        scratch_shapes=[pltpu.VMEM((MOE_TOP_K, tc, d), F32), pltpu.SemaphoreType.DMA(())],
        compiler_params=_params(("arbitrary",)),
        name="moe_combine",
    )(pos.reshape(n_t, 1, tc * MOE_TOP_K), gate, y_rows, y_shared, x,
      g.reshape(g.shape[0], 1, d), b.reshape(b.shape[0], 1, d))


def _moe_plan(eidx, rank, counts, tm, nf):
    t, k = eidx.shape
    n_exp = counts.shape[0]
    n_blk = -(-(t * k) // tm) + n_exp
    nb_e = (counts + tm - 1) // tm
    blk_end = jnp.cumsum(nb_e)
    blk_start = blk_end - nb_e
    n_used = blk_end[-1]
    row0 = blk_start * tm
    pos = _lookup(row0, eidx) + rank
    blk = jnp.arange(n_blk, dtype=jnp.int32)
    blk_c = jnp.minimum(blk, n_used - 1)
    blk_e = jnp.sum((blk_c[:, None] >= blk_end[None, :]).astype(jnp.int32), axis=1)
    blk_valid = (blk < n_used).astype(jnp.int32)
    blk_first = (blk_c == _lookup(blk_start, blk_e)).astype(jnp.int32)
    n_items = n_blk * nf
    item_end = jnp.cumsum(nb_e * nf)
    it = jnp.minimum(jnp.arange(n_items, dtype=jnp.int32), item_end[-1] - 1)
    it_e = jnp.sum((it[:, None] >= item_end[None, :]).astype(jnp.int32), axis=1)
    local = it - _lookup(item_end - nb_e * nf, it_e)
    nbe = jnp.maximum(_lookup(nb_e, it_e), 1)
    it_f = local // nbe
    it_r = local % nbe
    it_blk = _lookup(blk_start, it_e) + it_r
    i32 = lambda a: a.astype(jnp.int32)
    return dict(pos=i32(pos), pad_lo=i32(row0 + counts), pad_n=i32(nb_e * tm - counts), n_rows=n_blk * tm,
                blk_e=i32(blk_e), blk_first=blk_first, blk_valid=blk_valid, blk_idx=i32(blk_c),
                it_e=i32(it_e), it_f=i32(it_f), it_blk=i32(it_blk), it_first=i32(it_r == 0),
                it_valid=i32(jnp.arange(n_items) < item_end[-1]))


def _ff_tile(ff):
    return _tile(ff, (384, 256, 128))


def moe_layer(x, x_bf, layer, w_router, b_router, w_gu, w_dn, ws_gu, ws_dn, ln_g, ln_b, alpha):
    t, d = x.shape
    ff = w_dn.shape[2]
    tm = _tile(t, (256, 128, 64, 32, 16))
    tf = _ff_tile(ff)
    nf = ff // tf
    eidx, gate, rank, counts = moe_router(x, w_router, b_router, layer)
    plan = _moe_plan(eidx, rank, counts[0], tm, nf)
    x_rows = dispatch_rows(x, plan['pos'], plan['pad_lo'], plan['pad_n'], plan['n_rows'], tm)
    h_rows = expert_gate_up(x_rows, w_gu, layer, plan['it_e'], plan['it_f'], plan['it_blk'], plan['it_first'],
                            plan['it_valid'], tm, tf)
    y_rows = expert_down(h_rows, w_dn, layer, plan['blk_e'], plan['blk_first'], plan['blk_valid'],
                         plan['blk_idx'], tm)
    sff = ws_dn.shape[1]
    stf = _ff_tile(sff)
    snf = sff // stf
    nsb = t // tm
    zeros = lambda n: jnp.zeros((n,), jnp.int32)
    ones = lambda n: jnp.ones((n,), jnp.int32)
    s_blk = jnp.tile(jnp.arange(nsb, dtype=jnp.int32), snf)
    s_f = jnp.repeat(jnp.arange(snf, dtype=jnp.int32), nsb)
    s_first = (s_blk == 0).astype(jnp.int32)
    hs = expert_gate_up(x_bf, ws_gu[:, None], layer, zeros(nsb * snf), s_f, s_blk, s_first, ones(nsb * snf), tm, stf)
    sb = jnp.arange(nsb, dtype=jnp.int32)
    ys = expert_down(hs, ws_dn[:, None], layer, zeros(nsb), (sb == 0).astype(jnp.int32), ones(nsb), sb, tm)
    return moe_combine(plan['pos'], gate, y_rows, ys, x, ln_g, ln_b, layer, alpha)


def kernel(x_prompt, x_sample, state_hgrn, cache_swa_k, cache_swa_v, state_conv, hg_w_in, hg_lb, hg_norm, hg_w_o, swa_w_qkv, swa_b_qkv, swa_sinks, swa_w_o, swa_b_o, cv_w_pw1, cv_b_pw1, cv_w_dw, cv_b_dw, cv_ln_g, cv_ln_b, cv_w_pw2, cv_b_pw2, ln_mix_g, ln_mix_b, ln_ffn_g, ln_ffn_b, moe_w_router, moe_b_router, moe_w_gu, moe_w_dn, moe_ws_gu, moe_ws_dn):
    bp, lp, d = x_prompt.shape
    bs, ls, _ = x_sample.shape
    tp, ts = bp * lp, bs * ls
    depth = ln_mix_g.shape[0]
    alpha = (2 * depth) ** 0.25
    kvd = SWA_KV_HEADS * SWA_HEAD_DIM
    buf_len = cache_swa_k.shape[2]
    cw = cv_w_dw.shape[1]

    lb_p = jax.nn.softmax(hg_lb.astype(F32), axis=0)
    lb_all = jnp.concatenate([jnp.zeros_like(lb_p[:1]), jnp.cumsum(lb_p[1:], axis=0)], axis=0)

    x = jnp.concatenate([x_prompt.reshape(tp, d), x_sample.reshape(ts, d)], axis=0)
    x_bf = x.astype(BF16)
    hg_p, hg_s, k_p, k_s, v_p, v_s, cv_p, cv_s = [], [], [], [], [], [], [], []
    for layer in range(depth):
        kind, j = layer % N_MIXERS, layer // N_MIXERS
        if kind == 0:
            hcat = matmul(x_bf, hg_w_in, j)
            o, sp = hgrn_recurrence(hcat, lb_all[j], hg_norm[j], None, None,
                                    row0=0, nb=bp, seq=lp, chunk=math.gcd(lp, HG_CHUNK))
            o, ss = hgrn_recurrence(hcat, lb_all[j], hg_norm[j], state_hgrn[j], o,
                                    row0=tp, nb=bs, seq=ls, chunk=math.gcd(ls, HG_CHUNK))
            mix = matmul(o, hg_w_o, j)
            hg_p.append(sp)
            hg_s.append(ss)
        elif kind == 1:
            qkv = matmul(x_bf, swa_w_qkv, j, swa_b_qkv)
            o = swa_attention(qkv, None, None, swa_sinks[j], None, row0=0, nb=bp, seq=lp, d=d)
            ck = cache_swa_k[j].reshape(bs * buf_len, kvd)
            cv = cache_swa_v[j].reshape(bs * buf_len, kvd)
            o = swa_attention(qkv, ck, cv, swa_sinks[j], o, row0=tp, nb=bs, seq=ls, d=d)
            mix = matmul(o, swa_w_o, j, swa_b_o)
            k_new = qkv[:, d:d + kvd]
            v_new = qkv[:, d + kvd:]
            tail = lambda a: a[:tp].reshape(bp, lp, SWA_KV_HEADS, SWA_HEAD_DIM)[:, lp - buf_len:]
            k_p.append(tail(k_new))
            v_p.append(tail(v_new))
            new = lambda a: a[tp:].reshape(bs, ls, SWA_KV_HEADS, SWA_HEAD_DIM)
            k_s.append(jnp.concatenate([cache_swa_k[j], new(k_new)], axis=1)[:, ls:])
            v_s.append(jnp.concatenate([cache_swa_v[j], new(v_new)], axis=1)[:, ls:])
        else:
            hcat = matmul(x_bf, cv_w_pw1, j, cv_b_pw1)
            cargs = (cv_w_dw, cv_b_dw, cv_ln_g, cv_ln_b, j)
            y, u = conv_block(hcat, None, *cargs, None, None, row0=0, nb=bp, seq=lp)
            keep = cw - 1
            hist = jnp.pad(state_conv[j], ((0, 0), (CONV_HALO - keep, 0), (0, 0)))
            y, u = conv_block(hcat, hist, *cargs, y, u, row0=tp, nb=bs, seq=ls)
            mix = matmul(y, cv_w_pw2, j, cv_b_pw2)
            cv_p.append(u[:tp].reshape(bp, lp, d)[:, lp - keep:])
            cv_s.append(jnp.concatenate([state_conv[j], u[tp:].reshape(bs, ls, d)], axis=1)[:, ls:])
        x, x_bf = ln_residual(x, mix, ln_mix_g, ln_mix_b, layer, alpha)
        x, x_bf = moe_layer(x, x_bf, layer, moe_w_router, moe_b_router, moe_w_gu, moe_w_dn,
                            moe_ws_gu, moe_ws_dn, ln_ffn_g, ln_ffn_b, alpha)
    return (x[:tp].reshape(bp, lp, d), x[tp:].reshape(bs, ls, d),
            jnp.stack(hg_p), jnp.stack(hg_s), jnp.stack(k_p), jnp.stack(k_s), jnp.stack(v_p), jnp.stack(v_s),
            jnp.stack(cv_p), jnp.stack(cv_s))
```

```python
import functools
import math

import jax
import jax.numpy as jnp
from jax import lax
from jax.experimental import pallas as pl
from jax.experimental.pallas import tpu as pltpu

F32 = jnp.float32
BF16 = jnp.bfloat16

N_MIXERS = 3
HG_EXPAND = 128
HG_CHUNK = 32
SWA_WINDOW = 128
SWA_HEAD_DIM = 64
SWA_KV_HEADS = 8
MOE_TOP_K = 8
MOE_GROUPS = 8
MOE_TOPK_GROUPS = 4
MOE_ROUTED_SCALE = 2.5
LN_EPS = 1e-5
RMS_EPS = 1e-6

LANES = 128
SUBLANES = 8
VMEM_LIMIT_BYTES = 56 * 1024 * 1024


def _tile(dim, prefs):
    for p in prefs:
        if dim % p == 0:
            return p
    return dim


def _params(sem):
    return pltpu.CompilerParams(dimension_semantics=sem, vmem_limit_bytes=VMEM_LIMIT_BYTES)


def _sigmoid(x):
    return 1.0 / (1.0 + jnp.exp(-x))


def _lookup(table, idx):
    n = table.shape[0]
    hit = idx[..., None] == jnp.arange(n, dtype=jnp.int32)
    return jnp.sum(jnp.where(hit, table, 0), axis=-1)


def _mm_kernel(*refs, has_bias):
    if has_bias:
        x_ref, w_ref, b_ref, o_ref, wbf_ref = refs
    else:
        x_ref, w_ref, o_ref, wbf_ref = refs

    @pl.when(pl.program_id(1) == 0)
    def _():
        wbf_ref[...] = w_ref[...].astype(BF16)

    acc = jnp.dot(x_ref[...].astype(BF16), wbf_ref[...], preferred_element_type=F32)
    if has_bias:
        acc = acc + b_ref[...]
    o_ref[...] = acc.astype(o_ref.dtype)


def matmul(x, w, layer, b=None, out_dtype=F32):
    m, k = x.shape
    n = w.shape[2]
    tm = _tile(m, (768, 512, 256, 128, 64, 32, 16, 8))
    tn = _tile(n, (512, 256, 128))
    in_specs = [pl.BlockSpec((tm, k), lambda j, i: (i, 0)),
                pl.BlockSpec((None, k, tn), lambda j, i: (layer, 0, j))]
    args = [x, w]
    if b is not None:
        in_specs.append(pl.BlockSpec((None, 1, tn), lambda j, i: (layer, 0, j)))
        args.append(b.reshape(b.shape[0], 1, n))
    return pl.pallas_call(
        functools.partial(_mm_kernel, has_bias=b is not None),
        out_shape=jax.ShapeDtypeStruct((m, n), out_dtype),
        grid=(n // tn, m // tm),
        in_specs=in_specs,
        out_specs=pl.BlockSpec((tm, tn), lambda j, i: (i, j)),
        scratch_shapes=[pltpu.VMEM((k, tn), BF16)],
        compiler_params=_params(("arbitrary", "arbitrary")),
        name="matmul",
    )(*args)


def _layer_norm_rows(z, g, b):
    mu = jnp.mean(z, axis=-1, keepdims=True)
    zc = z - mu
    var = jnp.mean(zc * zc, axis=-1, keepdims=True)
    return zc * lax.rsqrt(var + LN_EPS) * g + b


def _pack_bf16_halves(y):
    h = y.shape[1] // 2
    lo = lax.bitcast_convert_type(y[:, :h].astype(BF16).astype(F32), jnp.uint32)
    hi = lax.bitcast_convert_type(y[:, h:].astype(BF16).astype(F32), jnp.uint32)
    return hi | (lo >> 16)


def _unpack_bf16_halves(u):
    lo = lax.bitcast_convert_type(u << 16, F32).astype(BF16)
    hi = lax.bitcast_convert_type(u & jnp.uint32(0xFFFF0000), F32).astype(BF16)
    return lo, hi


def _ln_res_kernel(x_ref, m_ref, g_ref, b_ref, o_ref, obf_ref, opk_ref, *, alpha):
    y = _layer_norm_rows(alpha * x_ref[...] + m_ref[...], g_ref[...], b_ref[...])
    o_ref[...] = y
    obf_ref[...] = y.astype(BF16)
    opk_ref[...] = _pack_bf16_halves(y)


def _vec_spec(d, layer, ngrid):
    if ngrid == 1:
        return pl.BlockSpec((None, 1, d), lambda i: (layer, 0, 0))
    return pl.BlockSpec((None, 1, d), lambda i, j: (layer, 0, 0))


def ln_residual(x, m, g, b, layer, alpha):
    t, d = x.shape
    tm = _tile(t, (256, 128, 64, 32, 16))
    row = pl.BlockSpec((tm, d), lambda i: (i, 0))
    vec = _vec_spec(d, layer, 1)
    return pl.pallas_call(
        functools.partial(_ln_res_kernel, alpha=alpha),
        out_shape=(jax.ShapeDtypeStruct((t, d), F32), jax.ShapeDtypeStruct((t, d), BF16),
                   jax.ShapeDtypeStruct((t, d // 2), jnp.uint32)),
        grid=(t // tm,),
        in_specs=[row, row, vec, vec],
        out_specs=(row, row, pl.BlockSpec((tm, d // 2), lambda i: (i, 0))),
        compiler_params=_params(("arbitrary",)),
        name="ln_residual",
    )(x, m, g.reshape(g.shape[0], 1, d), b.reshape(b.shape[0], 1, d))


def _hgrn_kernel(*refs, chunk, has_s0):
    if has_s0:
        q_ref, f_ref, i_ref, g_ref, lb_ref, gn_ref, s0_ref, o_ref, sout_ref, st_ref = refs
    else:
        q_ref, f_ref, i_ref, g_ref, lb_ref, gn_ref, o_ref, sout_ref, st_ref = refs
    c_id = pl.program_id(2)

    @pl.when(c_id == 0)
    def _():
        if has_s0:
            st_ref[...] = s0_ref[...].T
        else:
            st_ref[...] = jnp.zeros_like(st_ref)

    rb, dk = q_ref.shape
    nc = rb // chunk
    n_sub = max(chunk // SUBLANES, 1)
    sub = chunk // n_sub
    lb = lb_ref[...]
    log_lb = jnp.log(lb)
    log_1m_lb = jnp.log1p(-lb)
    hq = q_ref[...]
    hf = f_ref[...]
    v = i_ref[...]
    hg = g_ref[...]
    q = hq * _sigmoid(hq)
    log_sig = jnp.minimum(hf, 0.0) - jnp.log1p(jnp.exp(-jnp.abs(hf)))
    bt = log_1m_lb + log_sig
    log_f = jnp.maximum(log_lb, bt) + jnp.log1p(jnp.exp(-jnp.abs(log_lb - bt)))
    k = (1.0 - lb) * _sigmoid(-hf)

    tril = (lax.broadcasted_iota(jnp.int32, (chunk, chunk), 0)
            >= lax.broadcasted_iota(jnp.int32, (chunk, chunk), 1)).astype(F32)
    if nc > 1:
        logf_w = jnp.concatenate([log_f[c * chunk:(c + 1) * chunk, :] for c in range(nc)], axis=1)
    else:
        logf_w = log_f
    cum_w = jnp.dot(tril, logf_w, precision=lax.Precision.HIGHEST, preferred_element_type=F32)
    if nc > 1:
        cum = jnp.concatenate([cum_w[:, c * dk:(c + 1) * dk] for c in range(nc)], axis=0)
    else:
        cum = cum_w
    cum3 = cum.reshape(nc, chunk, dk)
    q3 = q.reshape(nc, chunk, dk)
    k3 = k.reshape(nc, chunk, dk)

    lane_t = lax.broadcasted_iota(jnp.int32, (nc, sub, chunk), 2)
    s_in_sub = lax.broadcasted_iota(jnp.int32, (nc, sub, dk), 1)
    att_t = [jnp.zeros((nc, sub, chunk), F32) for _ in range(n_sub)]
    for t in range(chunk):
        jt = t // sub
        qt = q3[:, t:t + 1, :]
        ct = cum3[:, t:t + 1, :]
        for j in range(jt + 1):
            rel = ct - cum3[:, j * sub:(j + 1) * sub, :]
            if j == jt:
                rel = jnp.where(s_in_sub <= (t - jt * sub), rel, -jnp.inf)
            p = (qt * jnp.exp(rel)) * k3[:, j * sub:(j + 1) * sub, :]
            r = jnp.sum(p, axis=-1, keepdims=True)
            att_t[j] = jnp.where(lane_t == t, r, att_t[j])

    last3 = cum3[:, chunk - 1:chunk, :]
    qe = (q * jnp.exp(cum)).astype(BF16)
    kdec = (k3 * jnp.exp(last3 - cum3)).reshape(rb, dk).astype(BF16)
    vb = v.astype(BF16)
    tn_dims = (((0,), (0,)), ((), ()))
    nt_dims = (((1,), (1,)), ((), ()))
    st = st_ref[...]
    o_parts = []
    for c in range(nc):
        rows = slice(c * chunk, (c + 1) * chunk)
        att_c = att_t[0][c] if n_sub == 1 else jnp.concatenate([a[c] for a in att_t], axis=0)
        o_c = lax.dot_general(qe[rows], st.astype(BF16), nt_dims, preferred_element_type=F32)
        o_c = o_c + lax.dot_general(att_c.astype(BF16), vb[rows], tn_dims, preferred_element_type=F32)
        inc = lax.dot_general(vb[rows], kdec[rows], tn_dims, preferred_element_type=F32)
        st = jnp.exp(cum[(c + 1) * chunk - 1:(c + 1) * chunk, :]) * st + inc
        o_parts.append(o_c)
    st_ref[...] = st
    o = o_parts[0] if nc == 1 else jnp.concatenate(o_parts, axis=0)
    o = o * lax.rsqrt(jnp.mean(o * o, axis=-1, keepdims=True) + RMS_EPS) * gn_ref[...] * (hg * _sigmoid(hg))
    o_ref[...] = o.astype(o_ref.dtype)

    @pl.when(c_id == pl.num_programs(2) - 1)
    def _():
        sout_ref[...] = st.T


def hgrn_recurrence(hcat, lb, gnorm, s0, o_prev, *, row0, nb, seq, chunk):
    t, d4 = hcat.shape
    d = d4 // 4
    hd = HG_EXPAND
    nh = d // hd
    rb = max(_tile(seq, (512, 256, 128, 64, 32, 16, 8)), chunk)
    nrb = seq // rb
    rblk0 = row0 // rb

    def col(off):
        return pl.BlockSpec((rb, hd), lambda b, h, c, off=off: (rblk0 + b * nrb + c, off + h))

    in_specs = [col(0), col(nh), col(2 * nh), col(3 * nh),
                pl.BlockSpec((1, hd), lambda b, h, c: (0, h)),
                pl.BlockSpec((1, hd), lambda b, h, c: (0, 0))]
    args = [hcat, hcat, hcat, hcat, lb.reshape(1, d), gnorm.reshape(1, hd)]
    if s0 is not None:
        in_specs.append(pl.BlockSpec((None, None, hd, hd), lambda b, h, c: (b, h, 0, 0)))
        args.append(s0)
    aliases = {}
    if o_prev is not None:
        in_specs.append(pl.BlockSpec(memory_space=pl.ANY))
        args.append(o_prev)
        aliases = {len(args) - 1: 0}

    def kern(*refs):
        if o_prev is not None:
            n_in = len(args)
            refs = refs[:n_in - 1] + refs[n_in:]
        _hgrn_kernel(*refs, chunk=chunk, has_s0=s0 is not None)

    return pl.pallas_call(
        kern,
        out_shape=(jax.ShapeDtypeStruct((t, d), F32), jax.ShapeDtypeStruct((nb, nh, hd, hd), F32)),
        grid=(nb, nh, nrb),
        in_specs=in_specs,
        out_specs=(pl.BlockSpec((rb, hd), lambda b, h, c: (rblk0 + b * nrb + c, h)),
                   pl.BlockSpec((None, None, hd, hd), lambda b, h, c: (b, h, 0, 0))),
        scratch_shapes=[pltpu.VMEM((hd, hd), F32)],
        input_output_aliases=aliases,
        compiler_params=_params(("arbitrary", "arbitrary", "arbitrary")),
        name="hgrn_recurrence",
    )(*args)


def _swa_kernel(q_ref, kp_ref, ko_ref, vp_ref, vo_ref, sink_ref, o_ref, *, group, first_block_has_no_prev):
    lq = q_ref.shape[0]
    sp_len = kp_ref.shape[0]
    so_len = ko_ref.shape[0]
    hd = SWA_HEAD_DIM
    scale = hd ** -0.5
    i_p = lax.broadcasted_iota(jnp.int32, (lq, sp_len), 0)
    j_p = lax.broadcasted_iota(jnp.int32, (lq, sp_len), 1)
    mask_p = j_p > i_p
    if first_block_has_no_prev:
        mask_p = jnp.logical_and(mask_p, pl.program_id(1) > 0)
    i_o = lax.broadcasted_iota(jnp.int32, (lq, so_len), 0)
    j_o = lax.broadcasted_iota(jnp.int32, (lq, so_len), 1)
    mask_o = j_o <= i_o
    nt = (((1,), (1,)), ((), ()))
    kv_per_block = kp_ref.shape[1] // hd
    heads = [(kv, kv * group + g) for kv in range(kv_per_block) for g in range(group)]
    kvs = []
    for kv in range(kv_per_block):
        cs = slice(kv * hd, (kv + 1) * hd)
        kvs.append((kp_ref[:, cs].astype(BF16), ko_ref[:, cs].astype(BF16),
                    vp_ref[:, cs].astype(BF16), vo_ref[:, cs].astype(BF16)))
    scores = []
    for kv, h in heads:
        qh = q_ref[:, h * hd:(h + 1) * hd].astype(BF16)
        s_p = lax.dot_general(qh, kvs[kv][0], nt, preferred_element_type=F32) * scale
        s_o = lax.dot_general(qh, kvs[kv][1], nt, preferred_element_type=F32) * scale
        scores.append((jnp.where(mask_p, s_p, -jnp.inf), jnp.where(mask_o, s_o, -jnp.inf)))
    probs = []
    for (kv, h), (s_p, s_o) in zip(heads, scores):
        sink = sink_ref[0:1, h:h + 1]
        m = jnp.maximum(jnp.maximum(jnp.max(s_p, axis=-1, keepdims=True),
                                    jnp.max(s_o, axis=-1, keepdims=True)), sink)
        p_p = jnp.exp(s_p - m)
        p_o = jnp.exp(s_o - m)
        den = (jnp.sum(p_p, axis=-1, keepdims=True) + jnp.sum(p_o, axis=-1, keepdims=True)
               + jnp.exp(sink - m))
        probs.append(((p_p / den).astype(BF16), (p_o / den).astype(BF16)))
    outs = []
    for (kv, h), (p_p, p_o) in zip(heads, probs):
        outs.append(jnp.dot(p_p, kvs[kv][2], preferred_element_type=F32)
                    + jnp.dot(p_o, kvs[kv][3], preferred_element_type=F32))
    for (kv, h), o in zip(heads, outs):
        o_ref[:, h * hd:(h + 1) * hd] = o.astype(o_ref.dtype)


def swa_attention(qkv, k_prev, v_prev, sinks, o_prev, *, row0, nb, seq, d):
    t = qkv.shape[0]
    hd = SWA_HEAD_DIM
    group = d // hd // SWA_KV_HEADS
    kvb = LANES // hd
    n_kvb = SWA_KV_HEADS // kvb
    qw = kvb * group * hd
    lq = min(seq, SWA_WINDOW)
    nq = seq // lq
    rblk0 = row0 // lq
    kcol0 = d // LANES
    vcol0 = (d + SWA_KV_HEADS * hd) // LANES
    is_prompt = k_prev is None
    own = lambda c0: pl.BlockSpec((lq, LANES), lambda b, n, p: (rblk0 + b * nq + n, c0 + p))
    if is_prompt:
        prev = lambda c0: pl.BlockSpec(
            (lq, LANES), lambda b, n, p: (rblk0 + b * nq + jnp.maximum(n - 1, 0), c0 + p))
        kp_spec, vp_spec = prev(kcol0), prev(vcol0)
        k_prev = v_prev = qkv
    else:
        kp_spec = vp_spec = pl.BlockSpec((SWA_WINDOW, LANES), lambda b, n, p: (b, p))
    in_specs = [pl.BlockSpec((lq, qw), lambda b, n, p: (rblk0 + b * nq + n, p)),
                kp_spec, own(kcol0), vp_spec, own(vcol0),
                pl.BlockSpec((None, 1, kvb * group), lambda b, n, p: (p, 0, 0))]
    args = [qkv, k_prev, qkv, v_prev, qkv, sinks.reshape(n_kvb, 1, kvb * group)]
    aliases = {}
    if o_prev is not None:
        in_specs.append(pl.BlockSpec(memory_space=pl.ANY))
        args.append(o_prev)
        aliases = {len(args) - 1: 0}

    def kern(*refs):
        if o_prev is not None:
            n_in = len(args)
            refs = refs[:n_in - 1] + refs[n_in:]
        _swa_kernel(*refs, group=group, first_block_has_no_prev=is_prompt)

    return pl.pallas_call(
        kern,
        out_shape=jax.ShapeDtypeStruct((t, d), F32),
        grid=(nb, nq, n_kvb),
        in_specs=in_specs,
        out_specs=pl.BlockSpec((lq, qw), lambda b, n, p: (rblk0 + b * nq + n, p)),
        input_output_aliases=aliases,
        compiler_params=_params(("arbitrary", "arbitrary", "arbitrary")),
        name="swa_attention",
    )(*args)


CONV_HALO = 32


def _conv_kernel(*refs, width, raw_halo):
    if raw_halo:
        a_ref, gt_ref, ha_ref, hgt_ref, wdw_ref, bdw_ref, lng_ref, lnb_ref, y_ref, u_ref, full_ref, acc_ref = refs
    else:
        a_ref, gt_ref, hu_ref, wdw_ref, bdw_ref, lng_ref, lnb_ref, y_ref, u_ref, full_ref, acc_ref = refs
    rb, c = a_ref.shape
    u = a_ref[...] * _sigmoid(gt_ref[...])
    u_ref[...] = u
    if raw_halo:
        hu = ha_ref[...] * _sigmoid(hgt_ref[...])
        hu = jnp.where(pl.program_id(1) > 0, hu, 0.0)
    else:
        hu = hu_ref[...]
    full_ref[0:CONV_HALO, :] = hu
    full_ref[CONV_HALO:CONV_HALO + rb, :] = u
    lead = CONV_HALO - (width - 1)
    rt = min(rb, 64)

    def col_tile(ct, carry):
        c0 = pl.multiple_of(ct * LANES, LANES)
        for r0 in range(0, rb, rt):
            acc = jnp.broadcast_to(bdw_ref[:, pl.ds(c0, LANES)], (rt, LANES))
            for w in range(width):
                acc = acc + full_ref[r0 + lead + w:r0 + lead + w + rt, pl.ds(c0, LANES)] * wdw_ref[w:w + 1, pl.ds(c0, LANES)]
            acc_ref[r0:r0 + rt, pl.ds(c0, LANES)] = acc
        return carry

    lax.fori_loop(0, c // LANES, col_tile, 0)
    y = _layer_norm_rows(acc_ref[...], lng_ref[...], lnb_ref[...])
    y_ref[...] = (y * _sigmoid(y)).astype(y_ref.dtype)


def conv_block(hcat, hist, w_dw, b_dw, ln_g, ln_b, layer, y_prev, u_prev, *, row0, nb, seq):
    t, c2 = hcat.shape
    c = c2 // 2
    nl, width, _ = w_dw.shape
    rb = _tile(seq, (256, 128, 64, 32, 16, 8))
    nrb = seq // rb
    rblk0 = row0 // rb
    raw_halo = hist is None
    main = lambda half: pl.BlockSpec((rb, c), lambda b, n: (rblk0 + b * nrb + n, half))
    vec = _vec_spec(c, layer, 2)
    in_specs = [main(0), main(1)]
    args = [hcat, hcat]
    if raw_halo:
        hpb = rb // CONV_HALO
        halo = lambda half: pl.BlockSpec(
            (CONV_HALO, c), lambda b, n: (jnp.maximum((rblk0 + b * nrb + n) * hpb - 1, 0), half))
        in_specs += [halo(0), halo(1)]
        args += [hcat, hcat]
    else:
        in_specs.append(pl.BlockSpec((None, CONV_HALO, c), lambda b, n: (b, 0, 0)))
        args.append(hist)
    in_specs += [pl.BlockSpec((None, width, c), lambda b, n: (layer, 0, 0)), vec, vec, vec]
    args += [w_dw, b_dw.reshape(nl, 1, c), ln_g.reshape(nl, 1, c), ln_b.reshape(nl, 1, c)]
    n_real = len(args)
    aliases = {}
    if y_prev is not None:
        in_specs += [pl.BlockSpec(memory_space=pl.ANY), pl.BlockSpec(memory_space=pl.ANY)]
        args += [y_prev, u_prev]
        aliases = {n_real: 0, n_real + 1: 1}

    def kern(*refs):
        if y_prev is not None:
            refs = refs[:n_real] + refs[n_real + 2:]
        _conv_kernel(*refs, width=width, raw_halo=raw_halo)

    out_blk = pl.BlockSpec((rb, c), lambda b, n: (rblk0 + b * nrb + n, 0))
    return pl.pallas_call(
        kern,
        out_shape=(jax.ShapeDtypeStruct((t, c), F32), jax.ShapeDtypeStruct((t, c), F32)),
        grid=(nb, nrb),
        in_specs=in_specs,
        out_specs=(out_blk, out_blk),
        scratch_shapes=[pltpu.VMEM((CONV_HALO + rb, c), F32), pltpu.VMEM((rb, c), F32)],
        input_output_aliases=aliases,
        compiler_params=_params(("arbitrary", "arbitrary")),
        name="conv_block",
    )(*args)


def _router_kernel(x_ref, w_ref, b_ref, eidx_ref, gate_ref, rank_ref, cnt_ref, run_ref):
    @pl.when(pl.program_id(0) == 0)
    def _():
        run_ref[...] = jnp.zeros_like(run_ref)

    logits = jnp.dot(x_ref[...], w_ref[...], precision=lax.Precision.HIGHEST, preferred_element_type=F32)
    scores = _sigmoid(logits)
    sel = scores + b_ref[...]
    tm, ne = sel.shape
    gsz = ne // MOE_GROUPS
    lane = lax.broadcasted_iota(jnp.int32, (tm, ne), 1).astype(F32)
    grp = lax.broadcasted_iota(jnp.int32, (tm, ne), 1) // gsz
    grp_f = grp.astype(F32)
    neg = -jnp.inf

    def first_argmax(v, idx_f):
        m = jnp.max(v, axis=-1, keepdims=True)
        return m, jnp.min(jnp.where(v == m, idx_f, float(ne)), axis=-1, keepdims=True)

    gscore = jnp.zeros((tm, ne), F32)
    for gi in range(MOE_GROUPS):
        in_g = grp == gi
        v = jnp.where(in_g, sel, neg)
        m1, i1 = first_argmax(v, lane)
        m2 = jnp.max(jnp.where(lane == i1, neg, v), axis=-1, keepdims=True)
        gscore = jnp.where(in_g, m1 + m2, gscore)
    gmask = jnp.zeros((tm, ne), jnp.bool_)
    for _ in range(MOE_TOPK_GROUPS):
        _, gi = first_argmax(gscore, grp_f)
        hit = grp_f == gi
        gmask = jnp.logical_or(gmask, hit)
        gscore = jnp.where(hit, neg, gscore)
    cand = jnp.where(gmask, sel, neg)
    k_iota = lax.broadcasted_iota(jnp.int32, (tm, MOE_TOP_K), 1)
    eidx = jnp.zeros((tm, MOE_TOP_K), F32)
    gate = jnp.zeros((tm, MOE_TOP_K), F32)
    chosen = jnp.zeros((tm, ne), F32)
    picks = []
    for kk in range(MOE_TOP_K):
        _, ei = first_argmax(cand, lane)
        hit = lane == ei
        wk = jnp.sum(jnp.where(hit, scores, 0.0), axis=-1, keepdims=True)
        cand = jnp.where(hit, neg, cand)
        chosen = jnp.where(hit, 1.0, chosen)
        eidx = jnp.where(k_iota == kk, ei, eidx)
        gate = jnp.where(k_iota == kk, wk, gate)
        picks.append(hit)
    gate = gate / jnp.sum(gate, axis=-1, keepdims=True) * MOE_ROUTED_SCALE
    eidx_ref[...] = eidx.astype(jnp.int32)
    gate_ref[...] = gate
    earlier = (lax.broadcasted_iota(jnp.int32, (tm, tm), 0)
               > lax.broadcasted_iota(jnp.int32, (tm, tm), 1)).astype(BF16)
    before = jnp.dot(earlier, chosen.astype(BF16), preferred_element_type=F32) + run_ref[...]
    rank = jnp.zeros((tm, MOE_TOP_K), F32)
    for kk in range(MOE_TOP_K):
        rk = jnp.sum(jnp.where(picks[kk], before, 0.0), axis=-1, keepdims=True)
        rank = jnp.where(k_iota == kk, rk, rank)
    rank_ref[...] = rank.astype(jnp.int32)
    run = run_ref[...] + jnp.sum(chosen, axis=0, keepdims=True)
    run_ref[...] = run
    cnt_ref[...] = run.astype(jnp.int32)


def moe_router(x, w_router, b_router, layer):
    t, d = x.shape
    nl, _, ne = w_router.shape
    tm = _tile(t, (256, 128, 64, 32, 16, 8))
    tok = lambda dt: jax.ShapeDtypeStruct((t, MOE_TOP_K), dt)
    tok_spec = pl.BlockSpec((tm, MOE_TOP_K), lambda i: (i, 0))
    return pl.pallas_call(
        _router_kernel,
        out_shape=(tok(jnp.int32), tok(F32), tok(jnp.int32), jax.ShapeDtypeStruct((1, ne), jnp.int32)),
        grid=(t // tm,),
        in_specs=[pl.BlockSpec((tm, d), lambda i: (i, 0)),
                  pl.BlockSpec((None, d, ne), lambda i: (layer, 0, 0)),
                  pl.BlockSpec((None, 1, ne), lambda i: (layer, 0, 0))],
        out_specs=(tok_spec, tok_spec, tok_spec, pl.BlockSpec((1, ne), lambda i: (0, 0))),
        scratch_shapes=[pltpu.VMEM((1, ne), F32)],
        compiler_params=_params(("arbitrary",)),
        name="moe_router",
    )(x, w_router, b_router.reshape(nl, 1, ne))


def _pad_pieces(tm, zr):
    pieces = [zr] * ((tm - 1) // zr)
    s = zr // 2
    while s >= SUBLANES:
        pieces.append(s)
        s //= 2
    return pieces


def _dispatch_kernel(padlo_ref, padn_ref, pos_ref, x_ref, o_hbm, zero_ref, sem, zsem, *, tm):
    n = pos_ref.shape[-1]

    def row_copy(t, dst):
        return pltpu.make_async_copy(x_ref.at[pl.ds(t, 1), :], o_hbm.at[pl.ds(dst, 1), :], sem)

    def issue(t, carry):
        for kk in range(MOE_TOP_K):
            row_copy(t, pos_ref[0, t * MOE_TOP_K + kk]).start()
        return carry

    lax.fori_loop(0, n // MOE_TOP_K, issue, 0)

    @pl.when(pl.program_id(0) == 0)
    def _():
        zero_ref[...] = jnp.zeros_like(zero_ref)

        def fill(e, wait):
            def piece(take, off, s):
                cp = pltpu.make_async_copy(zero_ref.at[pl.ds(0, s), :], o_hbm.at[pl.ds(off, s), :], zsem)

                @pl.when(take)
                def _():
                    if wait:
                        cp.wait()
                    else:
                        cp.start()

            off = padlo_ref[e]
            left = padn_ref[e]
            head = jnp.minimum((-off) & (SUBLANES - 1), left)
            for r in range(SUBLANES - 1):
                piece(head > r, off + r, 1)
            off = off + head
            left = left - head
            for s in _pad_pieces(tm, zero_ref.shape[0]):
                take = left >= s
                piece(take, pl.multiple_of(off, SUBLANES), s)
                step = jnp.where(take, s, 0)
                off = off + step
                left = left - step

        def fill_start(e, carry):
            fill(e, False)
            return carry

        def fill_wait(e, carry):
            fill(e, True)
            return carry

        ne = padlo_ref.shape[0]
        lax.fori_loop(0, ne, fill_start, 0)
        lax.fori_loop(0, ne, fill_wait, 0)

    def drain(t, carry):
        for _ in range(MOE_TOP_K):
            row_copy(t, 0).wait()
        return carry

    lax.fori_loop(0, n // MOE_TOP_K, drain, 0)


ZERO_ROWS = 32


def dispatch_rows(x, pos, pad_lo, pad_n, n_rows, tm):
    t, d = x.shape
    tt = _tile(t, (128, 64, 32, 16, 8))
    n_t = t // tt
    grid_spec = pltpu.PrefetchScalarGridSpec(
        num_scalar_prefetch=2,
        grid=(n_t,),
        in_specs=[pl.BlockSpec((None, 1, tt * MOE_TOP_K), lambda i, lo, nn: (i, 0, 0), memory_space=pltpu.SMEM),
                  pl.BlockSpec((tt, d), lambda i, lo, nn: (i, 0))],
        out_specs=pl.BlockSpec(memory_space=pl.ANY),
        scratch_shapes=[pltpu.VMEM((min(ZERO_ROWS, tm), d), x.dtype),
                        pltpu.SemaphoreType.DMA(()), pltpu.SemaphoreType.DMA(())],
    )
    return pl.pallas_call(
        functools.partial(_dispatch_kernel, tm=tm),
        out_shape=jax.ShapeDtypeStruct((n_rows, d), x.dtype),
        grid_spec=grid_spec,
        compiler_params=_params(("arbitrary",)),
        name="moe_dispatch",
    )(pad_lo, pad_n, pos.reshape(n_t, 1, tt * MOE_TOP_K), x)


def _expert_gu_kernel(e_ref, f_ref, blk_ref, first_ref, valid_ref, x_ref, wg_ref, wu_ref, h_ref, wcat_ref):
    i = pl.program_id(0)
    tf = wg_ref.shape[1]

    @pl.when(jnp.logical_and(valid_ref[i] == 1, first_ref[i] == 1))
    def _():
        wcat_ref[:, 0:tf] = wg_ref[...].astype(BF16)
        wcat_ref[:, tf:2 * tf] = wu_ref[...].astype(BF16)

    @pl.when(valid_ref[i] == 1)
    def _():
        if x_ref.dtype == jnp.uint32:
            half = x_ref.shape[1]
            x_lo, x_hi = _unpack_bf16_halves(x_ref[...])
            gu = (jnp.dot(x_lo, wcat_ref[0:half, :], preferred_element_type=F32)
                  + jnp.dot(x_hi, wcat_ref[half:2 * half, :], preferred_element_type=F32))
        else:
            gu = jnp.dot(x_ref[...].astype(BF16), wcat_ref[...], preferred_element_type=F32)
        g = gu[:, 0:tf]
        u = gu[:, tf:2 * tf]
        h_ref[...] = (g * _sigmoid(g) * u).astype(h_ref.dtype)


def expert_gate_up(x_rows, w_gu, layer, item_e, item_f, item_blk, item_first, item_valid, tm, tf):
    n_rows, xw = x_rows.shape
    d = w_gu.shape[2]
    ff = w_gu.shape[3] // 2
    n_items = item_e.shape[0]
    nf = ff // tf
    grid_spec = pltpu.PrefetchScalarGridSpec(
        num_scalar_prefetch=5,
        grid=(n_items,),
        in_specs=[pl.BlockSpec((tm, xw), lambda i, e, f, b, fi, va: (b[i], 0)),
                  pl.BlockSpec((None, None, d, tf), lambda i, e, f, b, fi, va: (layer, e[i], 0, f[i])),
                  pl.BlockSpec((None, None, d, tf), lambda i, e, f, b, fi, va: (layer, e[i], 0, nf + f[i]))],
        out_specs=pl.BlockSpec((tm, tf), lambda i, e, f, b, fi, va: (b[i], f[i])),
        scratch_shapes=[pltpu.VMEM((d, 2 * tf), BF16)],
    )
    return pl.pallas_call(
        _expert_gu_kernel,
        out_shape=jax.ShapeDtypeStruct((n_rows, ff), BF16),
        grid_spec=grid_spec,
        compiler_params=_params(("arbitrary",)),
        name="expert_gate_up",
    )(item_e, item_f, item_blk, item_first, item_valid, x_rows, w_gu, w_gu)


def _expert_dn_kernel(e_ref, first_ref, valid_ref, blk_ref, h_ref, w_ref, y_ref, wb_ref):
    i = pl.program_id(0)

    @pl.when(jnp.logical_and(valid_ref[i] == 1, first_ref[i] == 1))
    def _():
        wb_ref[...] = w_ref[...].astype(BF16)

    @pl.when(valid_ref[i] == 1)
    def _():
        y_ref[...] = jnp.dot(h_ref[...], wb_ref[...], preferred_element_type=F32)


def expert_down(h_rows, w_dn, layer, blk_e, blk_first, blk_valid, blk_idx, tm):
    n_rows, ff = h_rows.shape
    d = w_dn.shape[3]
    n_blk = blk_e.shape[0]
    grid_spec = pltpu.PrefetchScalarGridSpec(
        num_scalar_prefetch=4,
        grid=(n_blk,),
        in_specs=[pl.BlockSpec((tm, ff), lambda i, e, fi, va, b: (b[i], 0)),
                  pl.BlockSpec((None, None, ff, d), lambda i, e, fi, va, b: (layer, e[i], 0, 0))],
        out_specs=pl.BlockSpec((tm, d), lambda i, e, fi, va, b: (b[i], 0)),
        scratch_shapes=[pltpu.VMEM((ff, d), BF16)],
    )
    return pl.pallas_call(
        _expert_dn_kernel,
        out_shape=jax.ShapeDtypeStruct((n_rows, d), F32),
        grid_spec=grid_spec,
        compiler_params=_params(("arbitrary",)),
        name="expert_down",
    )(blk_e, blk_first, blk_valid, blk_idx, h_rows, w_dn)


def _combine_kernel(pos_ref, gate_ref, y_hbm, ysh_ref, x_ref, g_ref, b_ref, o_ref, obf_ref, buf_ref, sem, *, alpha):
    tc = x_ref.shape[0]

    def row_copy(t, kk, src):
        return pltpu.make_async_copy(y_hbm.at[pl.ds(src, 1), :], buf_ref.at[kk, pl.ds(t, 1), :], sem)

    def issue(t, carry):
        for kk in range(MOE_TOP_K):
            row_copy(t, kk, pos_ref[0, t * MOE_TOP_K + kk]).start()
        return carry

    lax.fori_loop(0, tc, issue, 0)

    def drain(t, carry):
        for kk in range(MOE_TOP_K):
            row_copy(t, kk, 0).wait()
        return carry

    lax.fori_loop(0, tc, drain, 0)
    y = ysh_ref[...]
    gate = gate_ref[...]
    for kk in range(MOE_TOP_K):
        y = y + gate[:, kk:kk + 1] * buf_ref[kk]
    out = _layer_norm_rows(alpha * x_ref[...] + y, g_ref[...], b_ref[...])
    o_ref[...] = out
    obf_ref[...] = out.astype(BF16)


def moe_combine(pos, gate, y_rows, y_shared, x, g, b, layer, alpha):
    t, d = x.shape
    tc = _tile(t, (32, 16))
    n_t = t // tc
    row = pl.BlockSpec((tc, d), lambda i: (i, 0))
    vec = _vec_spec(d, layer, 1)
    return pl.pallas_call(
        functools.partial(_combine_kernel, alpha=alpha),
        out_shape=(jax.ShapeDtypeStruct((t, d), F32), jax.ShapeDtypeStruct((t, d), BF16)),
        grid=(n_t,),
        in_specs=[pl.BlockSpec((None, 1, tc * MOE_TOP_K), lambda i: (i, 0, 0), memory_space=pltpu.SMEM),
                  pl.BlockSpec((tc, MOE_TOP_K), lambda i: (i, 0)),
                  pl.BlockSpec(memory_space=pl.ANY),
                  row, row, vec, vec],
        out_specs=(row, row),
        scratch_shapes=[pltpu.VMEM((MOE_TOP_K, tc, d), F32), pltpu.SemaphoreType.DMA(())],
        compiler_params=_params(("arbitrary",)),
        name="moe_combine",
    )(pos.reshape(n_t, 1, tc * MOE_TOP_K), gate, y_rows, y_shared, x,
      g.reshape(g.shape[0], 1, d), b.reshape(b.shape[0], 1, d))


def _moe_plan(eidx, rank, counts, tm, nf):
    t, k = eidx.shape
    n_exp = counts.shape[0]
    n_blk = -(-(t * k) // tm) + n_exp
    nb_e = (counts + tm - 1) // tm
    blk_end = jnp.cumsum(nb_e)
    blk_start = blk_end - nb_e
    n_used = blk_end[-1]
    row0 = blk_start * tm
    pos = _lookup(row0, eidx) + rank
    blk = jnp.arange(n_blk, dtype=jnp.int32)
    blk_c = jnp.minimum(blk, n_used - 1)
    blk_e = jnp.sum((blk_c[:, None] >= blk_end[None, :]).astype(jnp.int32), axis=1)
    blk_valid = (blk < n_used).astype(jnp.int32)
    blk_first = (blk_c == _lookup(blk_start, blk_e)).astype(jnp.int32)
    n_items = n_blk * nf
    item_end = jnp.cumsum(nb_e * nf)
    it = jnp.minimum(jnp.arange(n_items, dtype=jnp.int32), item_end[-1] - 1)
    it_e = jnp.sum((it[:, None] >= item_end[None, :]).astype(jnp.int32), axis=1)
    local = it - _lookup(item_end - nb_e * nf, it_e)
    nbe = jnp.maximum(_lookup(nb_e, it_e), 1)
    it_f = local // nbe
    it_r = local % nbe
    it_blk = _lookup(blk_start, it_e) + it_r
    i32 = lambda a: a.astype(jnp.int32)
    return dict(pos=i32(pos), pad_lo=i32(row0 + counts), pad_n=i32(nb_e * tm - counts), n_rows=n_blk * tm,
                blk_e=i32(blk_e), blk_first=blk_first, blk_valid=blk_valid, blk_idx=i32(blk_c),
                it_e=i32(it_e), it_f=i32(it_f), it_blk=i32(it_blk), it_first=i32(it_r == 0),
                it_valid=i32(jnp.arange(n_items) < item_end[-1]))


def _ff_tile(ff):
    return _tile(ff, (384, 256, 128))


def moe_layer(x, x_bf, x_pk, layer, w_router, b_router, w_gu, w_dn, ws_gu, ws_dn, ln_g, ln_b, alpha):
    t, d = x.shape
    ff = w_dn.shape[2]
    tm = _tile(t, (256, 128, 64, 32, 16))
    tf = _ff_tile(ff)
    nf = ff // tf
    eidx, gate, rank, counts = moe_router(x, w_router, b_router, layer)
    plan = _moe_plan(eidx, rank, counts[0], tm, nf)
    x_rows = dispatch_rows(x_pk, plan['pos'], plan['pad_lo'], plan['pad_n'], plan['n_rows'], tm)
    h_rows = expert_gate_up(x_rows, w_gu, layer, plan['it_e'], plan['it_f'], plan['it_blk'], plan['it_first'],
                            plan['it_valid'], tm, tf)
    y_rows = expert_down(h_rows, w_dn, layer, plan['blk_e'], plan['blk_first'], plan['blk_valid'],
                         plan['blk_idx'], tm)
    sff = ws_dn.shape[1]
    stf = _ff_tile(sff)
    snf = sff // stf
    nsb = t // tm
    zeros = lambda n: jnp.zeros((n,), jnp.int32)
    ones = lambda n: jnp.ones((n,), jnp.int32)
    s_blk = jnp.tile(jnp.arange(nsb, dtype=jnp.int32), snf)
    s_f = jnp.repeat(jnp.arange(snf, dtype=jnp.int32), nsb)
    s_first = (s_blk == 0).astype(jnp.int32)
    hs = expert_gate_up(x_bf, ws_gu[:, None], layer, zeros(nsb * snf), s_f, s_blk, s_first, ones(nsb * snf), tm, stf)
    sb = jnp.arange(nsb, dtype=jnp.int32)
    ys = expert_down(hs, ws_dn[:, None], layer, zeros(nsb), (sb == 0).astype(jnp.int32), ones(nsb), sb, tm)
    return moe_combine(plan['pos'], gate, y_rows, ys, x, ln_g, ln_b, layer, alpha)


def kernel(x_prompt, x_sample, state_hgrn, cache_swa_k, cache_swa_v, state_conv, hg_w_in, hg_lb, hg_norm, hg_w_o, swa_w_qkv, swa_b_qkv, swa_sinks, swa_w_o, swa_b_o, cv_w_pw1, cv_b_pw1, cv_w_dw, cv_b_dw, cv_ln_g, cv_ln_b, cv_w_pw2, cv_b_pw2, ln_mix_g, ln_mix_b, ln_ffn_g, ln_ffn_b, moe_w_router, moe_b_router, moe_w_gu, moe_w_dn, moe_ws_gu, moe_ws_dn):
    bp, lp, d = x_prompt.shape
    bs, ls, _ = x_sample.shape
    tp, ts = bp * lp, bs * ls
    depth = ln_mix_g.shape[0]
    alpha = (2 * depth) ** 0.25
    kvd = SWA_KV_HEADS * SWA_HEAD_DIM
    buf_len = cache_swa_k.shape[2]
    cw = cv_w_dw.shape[1]

    lb_p = jax.nn.softmax(hg_lb.astype(F32), axis=0)
    lb_all = jnp.concatenate([jnp.zeros_like(lb_p[:1]), jnp.cumsum(lb_p[1:], axis=0)], axis=0)

    x = jnp.concatenate([x_prompt.reshape(tp, d), x_sample.reshape(ts, d)], axis=0)
    x_bf = x.astype(BF16)
    hg_p, hg_s, k_p, k_s, v_p, v_s, cv_p, cv_s = [], [], [], [], [], [], [], []
    for layer in range(depth):
        kind, j = layer % N_MIXERS, layer // N_MIXERS
        if kind == 0:
            hcat = matmul(x_bf, hg_w_in, j)
            o, sp = hgrn_recurrence(hcat, lb_all[j], hg_norm[j], None, None,
                                    row0=0, nb=bp, seq=lp, chunk=math.gcd(lp, HG_CHUNK))
            o, ss = hgrn_recurrence(hcat, lb_all[j], hg_norm[j], state_hgrn[j], o,
                                    row0=tp, nb=bs, seq=ls, chunk=math.gcd(ls, HG_CHUNK))
            mix = matmul(o, hg_w_o, j)
            hg_p.append(sp)
            hg_s.append(ss)
        elif kind == 1:
            qkv = matmul(x_bf, swa_w_qkv, j, swa_b_qkv)
            o = swa_attention(qkv, None, None, swa_sinks[j], None, row0=0, nb=bp, seq=lp, d=d)
            ck = cache_swa_k[j].reshape(bs * buf_len, kvd)
            cv = cache_swa_v[j].reshape(bs * buf_len, kvd)
            o = swa_attention(qkv, ck, cv, swa_sinks[j], o, row0=tp, nb=bs, seq=ls, d=d)
            mix = matmul(o, swa_w_o, j, swa_b_o)
            k_new = qkv[:, d:d + kvd]
            v_new = qkv[:, d + kvd:]
            tail = lambda a: a[:tp].reshape(bp, lp, SWA_KV_HEADS, SWA_HEAD_DIM)[:, lp - buf_len:]
            k_p.append(tail(k_new))
            v_p.append(tail(v_new))
            new = lambda a: a[tp:].reshape(bs, ls, SWA_KV_HEADS, SWA_HEAD_DIM)
            k_s.append(jnp.concatenate([cache_swa_k[j], new(k_new)], axis=1)[:, ls:])
            v_s.append(jnp.concatenate([cache_swa_v[j], new(v_new)], axis=1)[:, ls:])
        else:
            hcat = matmul(x_bf, cv_w_pw1, j, cv_b_pw1)
            cargs = (cv_w_dw, cv_b_dw, cv_ln_g, cv_ln_b, j)
            y, u = conv_block(hcat, None, *cargs, None, None, row0=0, nb=bp, seq=lp)
            keep = cw - 1
            hist = jnp.pad(state_conv[j], ((0, 0), (CONV_HALO - keep, 0), (0, 0)))
            y, u = conv_block(hcat, hist, *cargs, y, u, row0=tp, nb=bs, seq=ls)
            mix = matmul(y, cv_w_pw2, j, cv_b_pw2)
            cv_p.append(u[:tp].reshape(bp, lp, d)[:, lp - keep:])
            cv_s.append(jnp.concatenate([state_conv[j], u[tp:].reshape(bs, ls, d)], axis=1)[:, ls:])
        x, x_bf, x_pk = ln_residual(x, mix, ln_mix_g, ln_mix_b, layer, alpha)
        x, x_bf = moe_layer(x, x_bf, x_pk, layer, moe_w_router, moe_b_router, moe_w_gu, moe_w_dn,
                            moe_ws_gu, moe_ws_dn, ln_ffn_g, ln_ffn_b, alpha)
    return (x[:tp].reshape(bp, lp, d), x[tp:].reshape(bs, ls, d),
            jnp.stack(hg_p), jnp.stack(hg_s), jnp.stack(k_p), jnp.stack(k_s), jnp.stack(v_p), jnp.stack(v_s),
            jnp.stack(cv_p), jnp.stack(cv_s))
```

```python
import functools
import math

import jax
import jax.numpy as jnp
from jax import lax
from jax.experimental import pallas as pl
from jax.experimental.pallas import tpu as pltpu

F32 = jnp.float32
BF16 = jnp.bfloat16

N_MIXERS = 3
HG_EXPAND = 128
HG_CHUNK = 32
HG_HEADS_PER_STEP = 4
SWA_WINDOW = 128
SWA_HEAD_DIM = 64
SWA_KV_HEADS = 8
MOE_TOP_K = 8
MOE_GROUPS = 8
MOE_TOPK_GROUPS = 4
MOE_ROUTED_SCALE = 2.5
LN_EPS = 1e-5
RMS_EPS = 1e-6

LANES = 128
SUBLANES = 8
VMEM_LIMIT_BYTES = 56 * 1024 * 1024


def _tile(dim, prefs):
    for p in prefs:
        if dim % p == 0:
            return p
    return dim


def _params(sem):
    return pltpu.CompilerParams(dimension_semantics=sem, vmem_limit_bytes=VMEM_LIMIT_BYTES)


def _sigmoid(x):
    return 1.0 / (1.0 + jnp.exp(-x))


def _lookup(table, idx):
    n = table.shape[0]
    hit = idx[..., None] == jnp.arange(n, dtype=jnp.int32)
    return jnp.sum(jnp.where(hit, table, 0), axis=-1)


def _mm_kernel(*refs, has_bias):
    if has_bias:
        x_ref, w_ref, b_ref, o_ref, wbf_ref = refs
    else:
        x_ref, w_ref, o_ref, wbf_ref = refs

    @pl.when(pl.program_id(1) == 0)
    def _():
        wbf_ref[...] = w_ref[...].astype(BF16)

    acc = jnp.dot(x_ref[...].astype(BF16), wbf_ref[...], preferred_element_type=F32)
    if has_bias:
        acc = acc + b_ref[...]
    o_ref[...] = acc.astype(o_ref.dtype)


def matmul(x, w, layer, b=None, out_dtype=F32):
    m, k = x.shape
    n = w.shape[2]
    tm = _tile(m, (768, 512, 256, 128, 64, 32, 16, 8))
    tn = _tile(n, (512, 256, 128))
    in_specs = [pl.BlockSpec((tm, k), lambda j, i: (i, 0)),
                pl.BlockSpec((None, k, tn), lambda j, i: (layer, 0, j))]
    args = [x, w]
    if b is not None:
        in_specs.append(pl.BlockSpec((None, 1, tn), lambda j, i: (layer, 0, j)))
        args.append(b.reshape(b.shape[0], 1, n))
    return pl.pallas_call(
        functools.partial(_mm_kernel, has_bias=b is not None),
        out_shape=jax.ShapeDtypeStruct((m, n), out_dtype),
        grid=(n // tn, m // tm),
        in_specs=in_specs,
        out_specs=pl.BlockSpec((tm, tn), lambda j, i: (i, j)),
        scratch_shapes=[pltpu.VMEM((k, tn), BF16)],
        compiler_params=_params(("arbitrary", "arbitrary")),
        name="matmul",
    )(*args)


def _layer_norm_rows(z, g, b):
    mu = jnp.mean(z, axis=-1, keepdims=True)
    zc = z - mu
    var = jnp.mean(zc * zc, axis=-1, keepdims=True)
    return zc * lax.rsqrt(var + LN_EPS) * g + b


def _pack_bf16_halves(y):
    h = y.shape[1] // 2
    lo = lax.bitcast_convert_type(y[:, :h].astype(BF16).astype(F32), jnp.uint32)
    hi = lax.bitcast_convert_type(y[:, h:].astype(BF16).astype(F32), jnp.uint32)
    return hi | (lo >> 16)


def _unpack_bf16_halves(u):
    lo = lax.bitcast_convert_type(u << 16, F32).astype(BF16)
    hi = lax.bitcast_convert_type(u & jnp.uint32(0xFFFF0000), F32).astype(BF16)
    return lo, hi


def _ln_res_kernel(x_ref, m_ref, g_ref, b_ref, o_ref, obf_ref, opk_ref, *, alpha):
    y = _layer_norm_rows(alpha * x_ref[...] + m_ref[...], g_ref[...], b_ref[...])
    o_ref[...] = y
    obf_ref[...] = y.astype(BF16)
    opk_ref[...] = _pack_bf16_halves(y)


def _vec_spec(d, layer, ngrid):
    if ngrid == 1:
        return pl.BlockSpec((None, 1, d), lambda i: (layer, 0, 0))
    return pl.BlockSpec((None, 1, d), lambda i, j: (layer, 0, 0))


def ln_residual(x, m, g, b, layer, alpha):
    t, d = x.shape
    tm = _tile(t, (256, 128, 64, 32, 16))
    row = pl.BlockSpec((tm, d), lambda i: (i, 0))
    vec = _vec_spec(d, layer, 1)
    return pl.pallas_call(
        functools.partial(_ln_res_kernel, alpha=alpha),
        out_shape=(jax.ShapeDtypeStruct((t, d), F32), jax.ShapeDtypeStruct((t, d), BF16),
                   jax.ShapeDtypeStruct((t, d // 2), jnp.uint32)),
        grid=(t // tm,),
        in_specs=[row, row, vec, vec],
        out_specs=(row, row, pl.BlockSpec((tm, d // 2), lambda i: (i, 0))),
        compiler_params=_params(("arbitrary",)),
        name="ln_residual",
    )(x, m, g.reshape(g.shape[0], 1, d), b.reshape(b.shape[0], 1, d))


def _hgrn_head(hq, hf, v, hg, lb, gn, st, chunk):
    rb, dk = hq.shape
    nc = rb // chunk
    n_sub = max(chunk // SUBLANES, 1)
    sub = chunk // n_sub
    log_lb = jnp.log(lb)
    log_1m_lb = jnp.log1p(-lb)
    q = hq * _sigmoid(hq)
    log_sig = jnp.minimum(hf, 0.0) - jnp.log1p(jnp.exp(-jnp.abs(hf)))
    bt = log_1m_lb + log_sig
    log_f = jnp.maximum(log_lb, bt) + jnp.log1p(jnp.exp(-jnp.abs(log_lb - bt)))
    k = (1.0 - lb) * _sigmoid(-hf)

    tril = (lax.broadcasted_iota(jnp.int32, (chunk, chunk), 0)
            >= lax.broadcasted_iota(jnp.int32, (chunk, chunk), 1)).astype(F32)
    if nc > 1:
        logf_w = jnp.concatenate([log_f[c * chunk:(c + 1) * chunk, :] for c in range(nc)], axis=1)
    else:
        logf_w = log_f
    cum_w = jnp.dot(tril, logf_w, precision=lax.Precision.HIGHEST, preferred_element_type=F32)
    if nc > 1:
        cum = jnp.concatenate([cum_w[:, c * dk:(c + 1) * dk] for c in range(nc)], axis=0)
    else:
        cum = cum_w
    cum3 = cum.reshape(nc, chunk, dk)
    q3 = q.reshape(nc, chunk, dk)
    k3 = k.reshape(nc, chunk, dk)

    lane_t = lax.broadcasted_iota(jnp.int32, (nc, sub, chunk), 2)
    s_in_sub = lax.broadcasted_iota(jnp.int32, (nc, sub, dk), 1)
    att_t = [jnp.zeros((nc, sub, chunk), F32) for _ in range(n_sub)]
    for t in range(chunk):
        jt = t // sub
        qt = q3[:, t:t + 1, :]
        ct = cum3[:, t:t + 1, :]
        for j in range(jt + 1):
            rel = ct - cum3[:, j * sub:(j + 1) * sub, :]
            if j == jt:
                rel = jnp.where(s_in_sub <= (t - jt * sub), rel, -jnp.inf)
            p = (qt * jnp.exp(rel)) * k3[:, j * sub:(j + 1) * sub, :]
            r = jnp.sum(p, axis=-1, keepdims=True)
            att_t[j] = jnp.where(lane_t == t, r, att_t[j])

    last3 = cum3[:, chunk - 1:chunk, :]
    qe = (q * jnp.exp(cum)).astype(BF16)
    kdec = (k3 * jnp.exp(last3 - cum3)).reshape(rb, dk).astype(BF16)
    vb = v.astype(BF16)
    tn_dims = (((0,), (0,)), ((), ()))
    nt_dims = (((1,), (1,)), ((), ()))
    o_parts = []
    for c in range(nc):
        rows = slice(c * chunk, (c + 1) * chunk)
        att_c = att_t[0][c] if n_sub == 1 else jnp.concatenate([a[c] for a in att_t], axis=0)
        o_c = lax.dot_general(qe[rows], st.astype(BF16), nt_dims, preferred_element_type=F32)
        o_c = o_c + lax.dot_general(att_c.astype(BF16), vb[rows], tn_dims, preferred_element_type=F32)
        inc = lax.dot_general(vb[rows], kdec[rows], tn_dims, preferred_element_type=F32)
        st = jnp.exp(cum[(c + 1) * chunk - 1:(c + 1) * chunk, :]) * st + inc
        o_parts.append(o_c)
    o = o_parts[0] if nc == 1 else jnp.concatenate(o_parts, axis=0)
    o = o * lax.rsqrt(jnp.mean(o * o, axis=-1, keepdims=True) + RMS_EPS) * gn * (hg * _sigmoid(hg))
    return o, st


def _hgrn_kernel(*refs, chunk, has_s0):
    if has_s0:
        q_ref, f_ref, i_ref, g_ref, lb_ref, gn_ref, s0_ref, o_ref, sout_ref, st_ref = refs
    else:
        q_ref, f_ref, i_ref, g_ref, lb_ref, gn_ref, o_ref, sout_ref, st_ref = refs
    c_id = pl.program_id(2)
    hb, dv, dk = st_ref.shape

    @pl.when(c_id == 0)
    def _():
        for hh in range(hb):
            st_ref[hh] = s0_ref[hh].T if has_s0 else jnp.zeros((dv, dk), F32)

    states = []
    for hh in range(hb):
        cs = slice(hh * dk, (hh + 1) * dk)
        o, st = _hgrn_head(q_ref[:, cs], f_ref[:, cs], i_ref[:, cs], g_ref[:, cs], lb_ref[:, cs], gn_ref[...],
                           st_ref[hh], chunk)
        st_ref[hh] = st
        o_ref[:, cs] = o.astype(o_ref.dtype)
        states.append(st)

    @pl.when(c_id == pl.num_programs(2) - 1)
    def _():
        for hh in range(hb):
            sout_ref[hh] = states[hh].T


def hgrn_recurrence(hcat, lb, gnorm, s0, o_prev, *, row0, nb, seq, chunk):
    t, d4 = hcat.shape
    d = d4 // 4
    hd = HG_EXPAND
    nh = d // hd
    rb = max(_tile(seq, (512, 256, 128, 64, 32, 16, 8)), chunk)
    nrb = seq // rb
    rblk0 = row0 // rb
    hb = HG_HEADS_PER_STEP if (rb == chunk and nh % HG_HEADS_PER_STEP == 0) else 1
    cw = hb * hd
    ngrp = nh // hb

    def col(off):
        return pl.BlockSpec((rb, cw), lambda b, h, c, off=off: (rblk0 + b * nrb + c, off + h))

    in_specs = [col(0), col(ngrp), col(2 * ngrp), col(3 * ngrp),
                pl.BlockSpec((1, cw), lambda b, h, c: (0, h)),
                pl.BlockSpec((1, hd), lambda b, h, c: (0, 0))]
    args = [hcat, hcat, hcat, hcat, lb.reshape(1, d), gnorm.reshape(1, hd)]
    if s0 is not None:
        in_specs.append(pl.BlockSpec((None, hb, hd, hd), lambda b, h, c: (b, h, 0, 0)))
        args.append(s0)
    aliases = {}
    if o_prev is not None:
        in_specs.append(pl.BlockSpec(memory_space=pl.ANY))
        args.append(o_prev)
        aliases = {len(args) - 1: 0}

    def kern(*refs):
        if o_prev is not None:
            n_in = len(args)
            refs = refs[:n_in - 1] + refs[n_in:]
        _hgrn_kernel(*refs, chunk=chunk, has_s0=s0 is not None)

    return pl.pallas_call(
        kern,
        out_shape=(jax.ShapeDtypeStruct((t, d), F32), jax.ShapeDtypeStruct((nb, nh, hd, hd), F32)),
        grid=(nb, ngrp, nrb),
        in_specs=in_specs,
        out_specs=(pl.BlockSpec((rb, cw), lambda b, h, c: (rblk0 + b * nrb + c, h)),
                   pl.BlockSpec((None, hb, hd, hd), lambda b, h, c: (b, h, 0, 0))),
        scratch_shapes=[pltpu.VMEM((hb, hd, hd), F32)],
        input_output_aliases=aliases,
        compiler_params=_params(("arbitrary", "arbitrary", "arbitrary")),
        name="hgrn_recurrence",
    )(*args)


def _swa_kernel(q_ref, kp_ref, ko_ref, vp_ref, vo_ref, sink_ref, o_ref, *, group, first_block_has_no_prev):
    lq = q_ref.shape[0]
    sp_len = kp_ref.shape[0]
    so_len = ko_ref.shape[0]
    hd = SWA_HEAD_DIM
    scale = hd ** -0.5
    i_p = lax.broadcasted_iota(jnp.int32, (lq, sp_len), 0)
    j_p = lax.broadcasted_iota(jnp.int32, (lq, sp_len), 1)
    mask_p = j_p > i_p
    if first_block_has_no_prev:
        mask_p = jnp.logical_and(mask_p, pl.program_id(1) > 0)
    i_o = lax.broadcasted_iota(jnp.int32, (lq, so_len), 0)
    j_o = lax.broadcasted_iota(jnp.int32, (lq, so_len), 1)
    mask_o = j_o <= i_o
    nt = (((1,), (1,)), ((), ()))
    kv_per_block = kp_ref.shape[1] // hd
    heads = [(kv, kv * group + g) for kv in range(kv_per_block) for g in range(group)]
    kvs = []
    for kv in range(kv_per_block):
        cs = slice(kv * hd, (kv + 1) * hd)
        kvs.append((kp_ref[:, cs].astype(BF16), ko_ref[:, cs].astype(BF16),
                    vp_ref[:, cs].astype(BF16), vo_ref[:, cs].astype(BF16)))
    scores = []
    for kv, h in heads:
        qh = q_ref[:, h * hd:(h + 1) * hd].astype(BF16)
        s_p = lax.dot_general(qh, kvs[kv][0], nt, preferred_element_type=F32) * scale
        s_o = lax.dot_general(qh, kvs[kv][1], nt, preferred_element_type=F32) * scale
        scores.append((jnp.where(mask_p, s_p, -jnp.inf), jnp.where(mask_o, s_o, -jnp.inf)))
    probs = []
    for (kv, h), (s_p, s_o) in zip(heads, scores):
        sink = sink_ref[0:1, h:h + 1]
        m = jnp.maximum(jnp.maximum(jnp.max(s_p, axis=-1, keepdims=True),
                                    jnp.max(s_o, axis=-1, keepdims=True)), sink)
        p_p = jnp.exp(s_p - m)
        p_o = jnp.exp(s_o - m)
        den = (jnp.sum(p_p, axis=-1, keepdims=True) + jnp.sum(p_o, axis=-1, keepdims=True)
               + jnp.exp(sink - m))
        probs.append(((p_p / den).astype(BF16), (p_o / den).astype(BF16)))
    outs = []
    for (kv, h), (p_p, p_o) in zip(heads, probs):
        outs.append(jnp.dot(p_p, kvs[kv][2], preferred_element_type=F32)
                    + jnp.dot(p_o, kvs[kv][3], preferred_element_type=F32))
    for (kv, h), o in zip(heads, outs):
        o_ref[:, h * hd:(h + 1) * hd] = o.astype(o_ref.dtype)


def swa_attention(qkv, k_prev, v_prev, sinks, o_prev, *, row0, nb, seq, d):
    t = qkv.shape[0]
    hd = SWA_HEAD_DIM
    group = d // hd // SWA_KV_HEADS
    kvb = LANES // hd
    n_kvb = SWA_KV_HEADS // kvb
    qw = kvb * group * hd
    lq = min(seq, SWA_WINDOW)
    nq = seq // lq
    rblk0 = row0 // lq
    kcol0 = d // LANES
    vcol0 = (d + SWA_KV_HEADS * hd) // LANES
    is_prompt = k_prev is None
    own = lambda c0: pl.BlockSpec((lq, LANES), lambda b, n, p: (rblk0 + b * nq + n, c0 + p))
    if is_prompt:
        prev = lambda c0: pl.BlockSpec(
            (lq, LANES), lambda b, n, p: (rblk0 + b * nq + jnp.maximum(n - 1, 0), c0 + p))
        kp_spec, vp_spec = prev(kcol0), prev(vcol0)
        k_prev = v_prev = qkv
    else:
        kp_spec = vp_spec = pl.BlockSpec((SWA_WINDOW, LANES), lambda b, n, p: (b, p))
    in_specs = [pl.BlockSpec((lq, qw), lambda b, n, p: (rblk0 + b * nq + n, p)),
                kp_spec, own(kcol0), vp_spec, own(vcol0),
                pl.BlockSpec((None, 1, kvb * group), lambda b, n, p: (p, 0, 0))]
    args = [qkv, k_prev, qkv, v_prev, qkv, sinks.reshape(n_kvb, 1, kvb * group)]
    aliases = {}
    if o_prev is not None:
        in_specs.append(pl.BlockSpec(memory_space=pl.ANY))
        args.append(o_prev)
        aliases = {len(args) - 1: 0}

    def kern(*refs):
        if o_prev is not None:
            n_in = len(args)
            refs = refs[:n_in - 1] + refs[n_in:]
        _swa_kernel(*refs, group=group, first_block_has_no_prev=is_prompt)

    return pl.pallas_call(
        kern,
        out_shape=jax.ShapeDtypeStruct((t, d), F32),
        grid=(nb, nq, n_kvb),
        in_specs=in_specs,
        out_specs=pl.BlockSpec((lq, qw), lambda b, n, p: (rblk0 + b * nq + n, p)),
        input_output_aliases=aliases,
        compiler_params=_params(("arbitrary", "arbitrary", "arbitrary")),
        name="swa_attention",
    )(*args)


CONV_HALO = 32


def _conv_kernel(*refs, width, raw_halo):
    if raw_halo:
        a_ref, gt_ref, ha_ref, hgt_ref, wdw_ref, bdw_ref, lng_ref, lnb_ref, y_ref, u_ref, full_ref, acc_ref = refs
    else:
        a_ref, gt_ref, hu_ref, wdw_ref, bdw_ref, lng_ref, lnb_ref, y_ref, u_ref, full_ref, acc_ref = refs
    rb, c = a_ref.shape
    u = a_ref[...] * _sigmoid(gt_ref[...])
    u_ref[...] = u
    if raw_halo:
        hu = ha_ref[...] * _sigmoid(hgt_ref[...])
        hu = jnp.where(pl.program_id(1) > 0, hu, 0.0)
    else:
        hu = hu_ref[...]
    full_ref[0:CONV_HALO, :] = hu
    full_ref[CONV_HALO:CONV_HALO + rb, :] = u
    lead = CONV_HALO - (width - 1)
    rt = min(rb, 64)

    def col_tile(ct, carry):
        c0 = pl.multiple_of(ct * LANES, LANES)
        for r0 in range(0, rb, rt):
            acc = jnp.broadcast_to(bdw_ref[:, pl.ds(c0, LANES)], (rt, LANES))
            for w in range(width):
                acc = acc + full_ref[r0 + lead + w:r0 + lead + w + rt, pl.ds(c0, LANES)] * wdw_ref[w:w + 1, pl.ds(c0, LANES)]
            acc_ref[r0:r0 + rt, pl.ds(c0, LANES)] = acc
        return carry

    lax.fori_loop(0, c // LANES, col_tile, 0)
    y = _layer_norm_rows(acc_ref[...], lng_ref[...], lnb_ref[...])
    y_ref[...] = (y * _sigmoid(y)).astype(y_ref.dtype)


def conv_block(hcat, hist, w_dw, b_dw, ln_g, ln_b, layer, y_prev, u_prev, *, row0, nb, seq):
    t, c2 = hcat.shape
    c = c2 // 2
    nl, width, _ = w_dw.shape
    rb = _tile(seq, (256, 128, 64, 32, 16, 8))
    nrb = seq // rb
    rblk0 = row0 // rb
    raw_halo = hist is None
    main = lambda half: pl.BlockSpec((rb, c), lambda b, n: (rblk0 + b * nrb + n, half))
    vec = _vec_spec(c, layer, 2)
    in_specs = [main(0), main(1)]
    args = [hcat, hcat]
    if raw_halo:
        hpb = rb // CONV_HALO
        halo = lambda half: pl.BlockSpec(
            (CONV_HALO, c), lambda b, n: (jnp.maximum((rblk0 + b * nrb + n) * hpb - 1, 0), half))
        in_specs += [halo(0), halo(1)]
        args += [hcat, hcat]
    else:
        in_specs.append(pl.BlockSpec((None, CONV_HALO, c), lambda b, n: (b, 0, 0)))
        args.append(hist)
    in_specs += [pl.BlockSpec((None, width, c), lambda b, n: (layer, 0, 0)), vec, vec, vec]
    args += [w_dw, b_dw.reshape(nl, 1, c), ln_g.reshape(nl, 1, c), ln_b.reshape(nl, 1, c)]
    n_real = len(args)
    aliases = {}
    if y_prev is not None:
        in_specs += [pl.BlockSpec(memory_space=pl.ANY), pl.BlockSpec(memory_space=pl.ANY)]
        args += [y_prev, u_prev]
        aliases = {n_real: 0, n_real + 1: 1}

    def kern(*refs):
        if y_prev is not None:
            refs = refs[:n_real] + refs[n_real + 2:]
        _conv_kernel(*refs, width=width, raw_halo=raw_halo)

    out_blk = pl.BlockSpec((rb, c), lambda b, n: (rblk0 + b * nrb + n, 0))
    return pl.pallas_call(
        kern,
        out_shape=(jax.ShapeDtypeStruct((t, c), F32), jax.ShapeDtypeStruct((t, c), F32)),
        grid=(nb, nrb),
        in_specs=in_specs,
        out_specs=(out_blk, out_blk),
        scratch_shapes=[pltpu.VMEM((CONV_HALO + rb, c), F32), pltpu.VMEM((rb, c), F32)],
        input_output_aliases=aliases,
        compiler_params=_params(("arbitrary", "arbitrary")),
        name="conv_block",
    )(*args)


def _router_kernel(x_ref, w_ref, b_ref, eidx_ref, gate_ref, rank_ref, cnt_ref, run_ref):
    @pl.when(pl.program_id(0) == 0)
    def _():
        run_ref[...] = jnp.zeros_like(run_ref)

    logits = jnp.dot(x_ref[...], w_ref[...], precision=lax.Precision.HIGHEST, preferred_element_type=F32)
    scores = _sigmoid(logits)
    sel = scores + b_ref[...]
    tm, ne = sel.shape
    gsz = ne // MOE_GROUPS
    lane = lax.broadcasted_iota(jnp.int32, (tm, ne), 1).astype(F32)
    grp = lax.broadcasted_iota(jnp.int32, (tm, ne), 1) // gsz
    grp_f = grp.astype(F32)
    neg = -jnp.inf

    def first_argmax(v, idx_f):
        m = jnp.max(v, axis=-1, keepdims=True)
        return m, jnp.min(jnp.where(v == m, idx_f, float(ne)), axis=-1, keepdims=True)

    gscore = jnp.zeros((tm, ne), F32)
    for gi in range(MOE_GROUPS):
        in_g = grp == gi
        v = jnp.where(in_g, sel, neg)
        m1, i1 = first_argmax(v, lane)
        m2 = jnp.max(jnp.where(lane == i1, neg, v), axis=-1, keepdims=True)
        gscore = jnp.where(in_g, m1 + m2, gscore)
    gmask = jnp.zeros((tm, ne), jnp.bool_)
    for _ in range(MOE_TOPK_GROUPS):
        _, gi = first_argmax(gscore, grp_f)
        hit = grp_f == gi
        gmask = jnp.logical_or(gmask, hit)
        gscore = jnp.where(hit, neg, gscore)
    cand = jnp.where(gmask, sel, neg)
    k_iota = lax.broadcasted_iota(jnp.int32, (tm, MOE_TOP_K), 1)
    eidx = jnp.zeros((tm, MOE_TOP_K), F32)
    gate = jnp.zeros((tm, MOE_TOP_K), F32)
    chosen = jnp.zeros((tm, ne), F32)
    picks = []
    for kk in range(MOE_TOP_K):
        _, ei = first_argmax(cand, lane)
        hit = lane == ei
        wk = jnp.sum(jnp.where(hit, scores, 0.0), axis=-1, keepdims=True)
        cand = jnp.where(hit, neg, cand)
        chosen = jnp.where(hit, 1.0, chosen)
        eidx = jnp.where(k_iota == kk, ei, eidx)
        gate = jnp.where(k_iota == kk, wk, gate)
        picks.append(hit)
    gate = gate / jnp.sum(gate, axis=-1, keepdims=True) * MOE_ROUTED_SCALE
    eidx_ref[...] = eidx.astype(jnp.int32)
    gate_ref[...] = gate
    earlier = (lax.broadcasted_iota(jnp.int32, (tm, tm), 0)
               > lax.broadcasted_iota(jnp.int32, (tm, tm), 1)).astype(BF16)
    before = jnp.dot(earlier, chosen.astype(BF16), preferred_element_type=F32) + run_ref[...]
    rank = jnp.zeros((tm, MOE_TOP_K), F32)
    for kk in range(MOE_TOP_K):
        rk = jnp.sum(jnp.where(picks[kk], before, 0.0), axis=-1, keepdims=True)
        rank = jnp.where(k_iota == kk, rk, rank)
    rank_ref[...] = rank.astype(jnp.int32)
    run = run_ref[...] + jnp.sum(chosen, axis=0, keepdims=True)
    run_ref[...] = run
    cnt_ref[...] = run.astype(jnp.int32)


def moe_router(x, w_router, b_router, layer):
    t, d = x.shape
    nl, _, ne = w_router.shape
    tm = _tile(t, (256, 128, 64, 32, 16, 8))
    tok = lambda dt: jax.ShapeDtypeStruct((t, MOE_TOP_K), dt)
    tok_spec = pl.BlockSpec((tm, MOE_TOP_K), lambda i: (i, 0))
    return pl.pallas_call(
        _router_kernel,
        out_shape=(tok(jnp.int32), tok(F32), tok(jnp.int32), jax.ShapeDtypeStruct((1, ne), jnp.int32)),
        grid=(t // tm,),
        in_specs=[pl.BlockSpec((tm, d), lambda i: (i, 0)),
                  pl.BlockSpec((None, d, ne), lambda i: (layer, 0, 0)),
                  pl.BlockSpec((None, 1, ne), lambda i: (layer, 0, 0))],
        out_specs=(tok_spec, tok_spec, tok_spec, pl.BlockSpec((1, ne), lambda i: (0, 0))),
        scratch_shapes=[pltpu.VMEM((1, ne), F32)],
        compiler_params=_params(("arbitrary",)),
        name="moe_router",
    )(x, w_router, b_router.reshape(nl, 1, ne))


def _pad_pieces(tm, zr):
    pieces = [zr] * ((tm - 1) // zr)
    s = zr // 2
    while s >= SUBLANES:
        pieces.append(s)
        s //= 2
    return pieces


def _dispatch_kernel(padlo_ref, padn_ref, pos_ref, x_ref, o_hbm, zero_ref, sem, zsem, *, tm):
    n = pos_ref.shape[-1]

    def row_copy(t, dst):
        return pltpu.make_async_copy(x_ref.at[pl.ds(t, 1), :], o_hbm.at[pl.ds(dst, 1), :], sem)

    def issue(t, carry):
        for kk in range(MOE_TOP_K):
            row_copy(t, pos_ref[0, t * MOE_TOP_K + kk]).start()
        return carry

    lax.fori_loop(0, n // MOE_TOP_K, issue, 0)

    @pl.when(pl.program_id(0) == 0)
    def _():
        zero_ref[...] = jnp.zeros_like(zero_ref)

        def fill(e, wait):
            def piece(take, off, s):
                cp = pltpu.make_async_copy(zero_ref.at[pl.ds(0, s), :], o_hbm.at[pl.ds(off, s), :], zsem)

                @pl.when(take)
                def _():
                    if wait:
                        cp.wait()
                    else:
                        cp.start()

            off = padlo_ref[e]
            left = padn_ref[e]
            head = jnp.minimum((-off) & (SUBLANES - 1), left)
            for r in range(SUBLANES - 1):
                piece(head > r, off + r, 1)
            off = off + head
            left = left - head
            for s in _pad_pieces(tm, zero_ref.shape[0]):
                take = left >= s
                piece(take, pl.multiple_of(off, SUBLANES), s)
                step = jnp.where(take, s, 0)
                off = off + step
                left = left - step

        def fill_start(e, carry):
            fill(e, False)
            return carry

        def fill_wait(e, carry):
            fill(e, True)
            return carry

        ne = padlo_ref.shape[0]
        lax.fori_loop(0, ne, fill_start, 0)
        lax.fori_loop(0, ne, fill_wait, 0)

    def drain(t, carry):
        for _ in range(MOE_TOP_K):
            row_copy(t, 0).wait()
        return carry

    lax.fori_loop(0, n // MOE_TOP_K, drain, 0)


ZERO_ROWS = 32


def dispatch_rows(x, pos, pad_lo, pad_n, n_rows, tm):
    t, d = x.shape
    tt = _tile(t, (128, 64, 32, 16, 8))
    n_t = t // tt
    grid_spec = pltpu.PrefetchScalarGridSpec(
        num_scalar_prefetch=2,
        grid=(n_t,),
        in_specs=[pl.BlockSpec((None, 1, tt * MOE_TOP_K), lambda i, lo, nn: (i, 0, 0), memory_space=pltpu.SMEM),
                  pl.BlockSpec((tt, d), lambda i, lo, nn: (i, 0))],
        out_specs=pl.BlockSpec(memory_space=pl.ANY),
        scratch_shapes=[pltpu.VMEM((min(ZERO_ROWS, tm), d), x.dtype),
                        pltpu.SemaphoreType.DMA(()), pltpu.SemaphoreType.DMA(())],
    )
    return pl.pallas_call(
        functools.partial(_dispatch_kernel, tm=tm),
        out_shape=jax.ShapeDtypeStruct((n_rows, d), x.dtype),
        grid_spec=grid_spec,
        compiler_params=_params(("arbitrary",)),
        name="moe_dispatch",
    )(pad_lo, pad_n, pos.reshape(n_t, 1, tt * MOE_TOP_K), x)


def _expert_gu_kernel(e_ref, f_ref, blk_ref, first_ref, valid_ref, x_ref, wg_ref, wu_ref, h_ref, wcat_ref):
    i = pl.program_id(0)
    tf = wg_ref.shape[1]

    @pl.when(jnp.logical_and(valid_ref[i] == 1, first_ref[i] == 1))
    def _():
        wcat_ref[:, 0:tf] = wg_ref[...].astype(BF16)
        wcat_ref[:, tf:2 * tf] = wu_ref[...].astype(BF16)

    @pl.when(valid_ref[i] == 1)
    def _():
        if x_ref.dtype == jnp.uint32:
            half = x_ref.shape[1]
            x_lo, x_hi = _unpack_bf16_halves(x_ref[...])
            gu = (jnp.dot(x_lo, wcat_ref[0:half, :], preferred_element_type=F32)
                  + jnp.dot(x_hi, wcat_ref[half:2 * half, :], preferred_element_type=F32))
        else:
            gu = jnp.dot(x_ref[...].astype(BF16), wcat_ref[...], preferred_element_type=F32)
        g = gu[:, 0:tf]
        u = gu[:, tf:2 * tf]
        h_ref[...] = (g * _sigmoid(g) * u).astype(h_ref.dtype)


def expert_gate_up(x_rows, w_gu, layer, item_e, item_f, item_blk, item_first, item_valid, tm, tf):
    n_rows, xw = x_rows.shape
    d = w_gu.shape[2]
    ff = w_gu.shape[3] // 2
    n_items = item_e.shape[0]
    nf = ff // tf
    grid_spec = pltpu.PrefetchScalarGridSpec(
        num_scalar_prefetch=5,
        grid=(n_items,),
        in_specs=[pl.BlockSpec((tm, xw), lambda i, e, f, b, fi, va: (b[i], 0)),
                  pl.BlockSpec((None, None, d, tf), lambda i, e, f, b, fi, va: (layer, e[i], 0, f[i])),
                  pl.BlockSpec((None, None, d, tf), lambda i, e, f, b, fi, va: (layer, e[i], 0, nf + f[i]))],
        out_specs=pl.BlockSpec((tm, tf), lambda i, e, f, b, fi, va: (b[i], f[i])),
        scratch_shapes=[pltpu.VMEM((d, 2 * tf), BF16)],
    )
    return pl.pallas_call(
        _expert_gu_kernel,
        out_shape=jax.ShapeDtypeStruct((n_rows, ff), BF16),
        grid_spec=grid_spec,
        compiler_params=_params(("arbitrary",)),
        name="expert_gate_up",
    )(item_e, item_f, item_blk, item_first, item_valid, x_rows, w_gu, w_gu)


def _expert_dn_kernel(e_ref, first_ref, valid_ref, blk_ref, h_ref, w_ref, y_ref, wb_ref):
    i = pl.program_id(0)

    @pl.when(jnp.logical_and(valid_ref[i] == 1, first_ref[i] == 1))
    def _():
        wb_ref[...] = w_ref[...].astype(BF16)

    @pl.when(valid_ref[i] == 1)
    def _():
        y_ref[...] = jnp.dot(h_ref[...], wb_ref[...], preferred_element_type=F32)


def expert_down(h_rows, w_dn, layer, blk_e, blk_first, blk_valid, blk_idx, tm):
    n_rows, ff = h_rows.shape
    d = w_dn.shape[3]
    n_blk = blk_e.shape[0]
    grid_spec = pltpu.PrefetchScalarGridSpec(
        num_scalar_prefetch=4,
        grid=(n_blk,),
        in_specs=[pl.BlockSpec((tm, ff), lambda i, e, fi, va, b: (b[i], 0)),
                  pl.BlockSpec((None, None, ff, d), lambda i, e, fi, va, b: (layer, e[i], 0, 0))],
        out_specs=pl.BlockSpec((tm, d), lambda i, e, fi, va, b: (b[i], 0)),
        scratch_shapes=[pltpu.VMEM((ff, d), BF16)],
    )
    return pl.pallas_call(
        _expert_dn_kernel,
        out_shape=jax.ShapeDtypeStruct((n_rows, d), F32),
        grid_spec=grid_spec,
        compiler_params=_params(("arbitrary",)),
        name="expert_down",
    )(blk_e, blk_first, blk_valid, blk_idx, h_rows, w_dn)


def _combine_kernel(pos_ref, pos_next_ref, gate_ref, y_hbm, ysh_ref, x_ref, g_ref, b_ref, o_ref, obf_ref,
                    buf_ref, sem, *, alpha):
    tc = x_ref.shape[0]
    i = pl.program_id(0)
    slot = i % 2

    def row_copy(s, t, kk, src):
        return pltpu.make_async_copy(y_hbm.at[pl.ds(src, 1), :], buf_ref.at[s, kk, pl.ds(t, 1), :], sem.at[s])

    def issue(p_ref, s):
        def body(t, carry):
            for kk in range(MOE_TOP_K):
                row_copy(s, t, kk, p_ref[0, t * MOE_TOP_K + kk]).start()
            return carry

        lax.fori_loop(0, tc, body, 0)

    @pl.when(i == 0)
    def _():
        issue(pos_ref, 0)

    @pl.when(i + 1 < pl.num_programs(0))
    def _():
        issue(pos_next_ref, 1 - slot)

    def drain(t, carry):
        for kk in range(MOE_TOP_K):
            row_copy(slot, t, kk, 0).wait()
        return carry

    lax.fori_loop(0, tc, drain, 0)
    y = ysh_ref[...]
    gate = gate_ref[...]
    for kk in range(MOE_TOP_K):
        y = y + gate[:, kk:kk + 1] * buf_ref[slot, kk]
    out = _layer_norm_rows(alpha * x_ref[...] + y, g_ref[...], b_ref[...])
    o_ref[...] = out
    obf_ref[...] = out.astype(BF16)


def moe_combine(pos, gate, y_rows, y_shared, x, g, b, layer, alpha):
    t, d = x.shape
    tc = _tile(t, (32, 16))
    n_t = t // tc
    row = pl.BlockSpec((tc, d), lambda i: (i, 0))
    vec = _vec_spec(d, layer, 1)
    pos_tiles = pos.reshape(n_t, 1, tc * MOE_TOP_K)
    return pl.pallas_call(
        functools.partial(_combine_kernel, alpha=alpha),
        out_shape=(jax.ShapeDtypeStruct((t, d), F32), jax.ShapeDtypeStruct((t, d), BF16)),
        grid=(n_t,),
        in_specs=[pl.BlockSpec((None, 1, tc * MOE_TOP_K), lambda i: (i, 0, 0), memory_space=pltpu.SMEM),
                  pl.BlockSpec((None, 1, tc * MOE_TOP_K), lambda i: (jnp.minimum(i + 1, n_t - 1), 0, 0),
                               memory_space=pltpu.SMEM),
                  pl.BlockSpec((tc, MOE_TOP_K), lambda i: (i, 0)),
                  pl.BlockSpec(memory_space=pl.ANY),
                  row, row, vec, vec],
        out_specs=(row, row),
        scratch_shapes=[pltpu.VMEM((2, MOE_TOP_K, tc, d), F32), pltpu.SemaphoreType.DMA((2,))],
        compiler_params=_params(("arbitrary",)),
        name="moe_combine",
    )(pos_tiles, pos_tiles, gate, y_rows, y_shared, x,
      g.reshape(g.shape[0], 1, d), b.reshape(b.shape[0], 1, d))


def _moe_plan(eidx, rank, counts, tm, nf):
    t, k = eidx.shape
    n_exp = counts.shape[0]
    n_blk = -(-(t * k) // tm) + n_exp
    nb_e = (counts + tm - 1) // tm
    blk_end = jnp.cumsum(nb_e)
    blk_start = blk_end - nb_e
    n_used = blk_end[-1]
    row0 = blk_start * tm
    pos = _lookup(row0, eidx) + rank
    blk = jnp.arange(n_blk, dtype=jnp.int32)
    blk_c = jnp.minimum(blk, n_used - 1)
    blk_e = jnp.sum((blk_c[:, None] >= blk_end[None, :]).astype(jnp.int32), axis=1)
    blk_valid = (blk < n_used).astype(jnp.int32)
    blk_first = (blk_c == _lookup(blk_start, blk_e)).astype(jnp.int32)
    n_items = n_blk * nf
    item_end = jnp.cumsum(nb_e * nf)
    it = jnp.minimum(jnp.arange(n_items, dtype=jnp.int32), item_end[-1] - 1)
    it_e = jnp.sum((it[:, None] >= item_end[None, :]).astype(jnp.int32), axis=1)
    local = it - _lookup(item_end - nb_e * nf, it_e)
    nbe = jnp.maximum(_lookup(nb_e, it_e), 1)
    it_f = local // nbe
    it_r = local % nbe
    it_blk = _lookup(blk_start, it_e) + it_r
    i32 = lambda a: a.astype(jnp.int32)
    return dict(pos=i32(pos), pad_lo=i32(row0 + counts), pad_n=i32(nb_e * tm - counts), n_rows=n_blk * tm,
                blk_e=i32(blk_e), blk_first=blk_first, blk_valid=blk_valid, blk_idx=i32(blk_c),
                it_e=i32(it_e), it_f=i32(it_f), it_blk=i32(it_blk), it_first=i32(it_r == 0),
                it_valid=i32(jnp.arange(n_items) < item_end[-1]))


def _ff_tile(ff):
    return _tile(ff, (384, 256, 128))


def moe_layer(x, x_bf, x_pk, layer, w_router, b_router, w_gu, w_dn, ws_gu, ws_dn, ln_g, ln_b, alpha):
    t, d = x.shape
    ff = w_dn.shape[2]
    tm = _tile(t, (256, 128, 64, 32, 16))
    tf = _ff_tile(ff)
    nf = ff // tf
    eidx, gate, rank, counts = moe_router(x, w_router, b_router, layer)
    plan = _moe_plan(eidx, rank, counts[0], tm, nf)
    x_rows = dispatch_rows(x_pk, plan['pos'], plan['pad_lo'], plan['pad_n'], plan['n_rows'], tm)
    h_rows = expert_gate_up(x_rows, w_gu, layer, plan['it_e'], plan['it_f'], plan['it_blk'], plan['it_first'],
                            plan['it_valid'], tm, tf)
    y_rows = expert_down(h_rows, w_dn, layer, plan['blk_e'], plan['blk_first'], plan['blk_valid'],
                         plan['blk_idx'], tm)
    sff = ws_dn.shape[1]
    stf = _ff_tile(sff)
    snf = sff // stf
    nsb = t // tm
    zeros = lambda n: jnp.zeros((n,), jnp.int32)
    ones = lambda n: jnp.ones((n,), jnp.int32)
    s_blk = jnp.tile(jnp.arange(nsb, dtype=jnp.int32), snf)
    s_f = jnp.repeat(jnp.arange(snf, dtype=jnp.int32), nsb)
    s_first = (s_blk == 0).astype(jnp.int32)
    hs = expert_gate_up(x_bf, ws_gu[:, None], layer, zeros(nsb * snf), s_f, s_blk, s_first, ones(nsb * snf), tm, stf)
    sb = jnp.arange(nsb, dtype=jnp.int32)
    ys = expert_down(hs, ws_dn[:, None], layer, zeros(nsb), (sb == 0).astype(jnp.int32), ones(nsb), sb, tm)
    return moe_combine(plan['pos'], gate, y_rows, ys, x, ln_g, ln_b, layer, alpha)


def kernel(x_prompt, x_sample, state_hgrn, cache_swa_k, cache_swa_v, state_conv, hg_w_in, hg_lb, hg_norm, hg_w_o, swa_w_qkv, swa_b_qkv, swa_sinks, swa_w_o, swa_b_o, cv_w_pw1, cv_b_pw1, cv_w_dw, cv_b_dw, cv_ln_g, cv_ln_b, cv_w_pw2, cv_b_pw2, ln_mix_g, ln_mix_b, ln_ffn_g, ln_ffn_b, moe_w_router, moe_b_router, moe_w_gu, moe_w_dn, moe_ws_gu, moe_ws_dn):
    bp, lp, d = x_prompt.shape
    bs, ls, _ = x_sample.shape
    tp, ts = bp * lp, bs * ls
    depth = ln_mix_g.shape[0]
    alpha = (2 * depth) ** 0.25
    kvd = SWA_KV_HEADS * SWA_HEAD_DIM
    buf_len = cache_swa_k.shape[2]
    cw = cv_w_dw.shape[1]

    lb_p = jax.nn.softmax(hg_lb.astype(F32), axis=0)
    lb_all = jnp.concatenate([jnp.zeros_like(lb_p[:1]), jnp.cumsum(lb_p[1:], axis=0)], axis=0)

    x = jnp.concatenate([x_prompt.reshape(tp, d), x_sample.reshape(ts, d)], axis=0)
    x_bf = x.astype(BF16)
    hg_p, hg_s, k_p, k_s, v_p, v_s, cv_p, cv_s = [], [], [], [], [], [], [], []
    for layer in range(depth):
        kind, j = layer % N_MIXERS, layer // N_MIXERS
        if kind == 0:
            hcat = matmul(x_bf, hg_w_in, j)
            o, sp = hgrn_recurrence(hcat, lb_all[j], hg_norm[j], None, None,
                                    row0=0, nb=bp, seq=lp, chunk=math.gcd(lp, HG_CHUNK))
            o, ss = hgrn_recurrence(hcat, lb_all[j], hg_norm[j], state_hgrn[j], o,
                                    row0=tp, nb=bs, seq=ls, chunk=math.gcd(ls, HG_CHUNK))
            mix = matmul(o, hg_w_o, j)
            hg_p.append(sp)
            hg_s.append(ss)
        elif kind == 1:
            qkv = matmul(x_bf, swa_w_qkv, j, swa_b_qkv)
            o = swa_attention(qkv, None, None, swa_sinks[j], None, row0=0, nb=bp, seq=lp, d=d)
            ck = cache_swa_k[j].reshape(bs * buf_len, kvd)
            cv = cache_swa_v[j].reshape(bs * buf_len, kvd)
            o = swa_attention(qkv, ck, cv, swa_sinks[j], o, row0=tp, nb=bs, seq=ls, d=d)
            mix = matmul(o, swa_w_o, j, swa_b_o)
            k_new = qkv[:, d:d + kvd]
            v_new = qkv[:, d + kvd:]
            tail = lambda a: a[:tp].reshape(bp, lp, SWA_KV_HEADS, SWA_HEAD_DIM)[:, lp - buf_len:]
            k_p.append(tail(k_new))
            v_p.append(tail(v_new))
            new = lambda a: a[tp:].reshape(bs, ls, SWA_KV_HEADS, SWA_HEAD_DIM)
            k_s.append(jnp.concatenate([cache_swa_k[j], new(k_new)], axis=1)[:, ls:])
            v_s.append(jnp.concatenate([cache_swa_v[j], new(v_new)], axis=1)[:, ls:])
        else:
            hcat = matmul(x_bf, cv_w_pw1, j, cv_b_pw1)
            cargs = (cv_w_dw, cv_b_dw, cv_ln_g, cv_ln_b, j)
            y, u = conv_block(hcat, None, *cargs, None, None, row0=0, nb=bp, seq=lp)
            keep = cw - 1
            hist = jnp.pad(state_conv[j], ((0, 0), (CONV_HALO - keep, 0), (0, 0)))
            y, u = conv_block(hcat, hist, *cargs, y, u, row0=tp, nb=bs, seq=ls)
            mix = matmul(y, cv_w_pw2, j, cv_b_pw2)
            cv_p.append(u[:tp].reshape(bp, lp, d)[:, lp - keep:])
            cv_s.append(jnp.concatenate([state_conv[j], u[tp:].reshape(bs, ls, d)], axis=1)[:, ls:])
        x, x_bf, x_pk = ln_residual(x, mix, ln_mix_g, ln_mix_b, layer, alpha)
        x, x_bf = moe_layer(x, x_bf, x_pk, layer, moe_w_router, moe_b_router, moe_w_gu, moe_w_dn,
                            moe_ws_gu, moe_ws_dn, ln_ffn_g, ln_ffn_b, alpha)
    return (x[:tp].reshape(bp, lp, d), x[tp:].reshape(bs, ls, d),
            jnp.stack(hg_p), jnp.stack(hg_s), jnp.stack(k_p), jnp.stack(k_s), jnp.stack(v_p), jnp.stack(v_s),
            jnp.stack(cv_p), jnp.stack(cv_s))
```

```python
import functools
import math

import jax
import jax.numpy as jnp
from jax import lax
from jax.experimental import pallas as pl
from jax.experimental.pallas import tpu as pltpu

F32 = jnp.float32
BF16 = jnp.bfloat16

N_MIXERS = 3
HG_EXPAND = 128
HG_CHUNK = 32
HG_HEADS_PER_STEP = 8
LOG2_E = math.log2(math.e)
SWA_WINDOW = 128
SWA_HEAD_DIM = 64
SWA_KV_HEADS = 8
MOE_TOP_K = 8
MOE_GROUPS = 8
MOE_TOPK_GROUPS = 4
MOE_ROUTED_SCALE = 2.5
LN_EPS = 1e-5
RMS_EPS = 1e-6

LANES = 128
SUBLANES = 8
VMEM_LIMIT_BYTES = 56 * 1024 * 1024


def _tile(dim, prefs):
    for p in prefs:
        if dim % p == 0:
            return p
    return dim


def _params(sem):
    return pltpu.CompilerParams(dimension_semantics=sem, vmem_limit_bytes=VMEM_LIMIT_BYTES)


def _sigmoid(x):
    return 1.0 / (1.0 + jnp.exp(-x))


def _lookup(table, idx):
    n = table.shape[0]
    hit = idx[..., None] == jnp.arange(n, dtype=jnp.int32)
    return jnp.sum(jnp.where(hit, table, 0), axis=-1)


def _mm_kernel(*refs, has_bias):
    if has_bias:
        x_ref, w_ref, b_ref, o_ref, wbf_ref = refs
    else:
        x_ref, w_ref, o_ref, wbf_ref = refs

    @pl.when(pl.program_id(1) == 0)
    def _():
        wbf_ref[...] = w_ref[...].astype(BF16)

    acc = jnp.dot(x_ref[...].astype(BF16), wbf_ref[...], preferred_element_type=F32)
    if has_bias:
        acc = acc + b_ref[...]
    o_ref[...] = acc.astype(o_ref.dtype)


def matmul(x, w, layer, b=None, out_dtype=F32):
    m, k = x.shape
    n = w.shape[2]
    tm = _tile(m, (768, 512, 256, 128, 64, 32, 16, 8))
    tn = _tile(n, (512, 256, 128))
    in_specs = [pl.BlockSpec((tm, k), lambda j, i: (i, 0)),
                pl.BlockSpec((None, k, tn), lambda j, i: (layer, 0, j))]
    args = [x, w]
    if b is not None:
        in_specs.append(pl.BlockSpec((None, 1, tn), lambda j, i: (layer, 0, j)))
        args.append(b.reshape(b.shape[0], 1, n))
    return pl.pallas_call(
        functools.partial(_mm_kernel, has_bias=b is not None),
        out_shape=jax.ShapeDtypeStruct((m, n), out_dtype),
        grid=(n // tn, m // tm),
        in_specs=in_specs,
        out_specs=pl.BlockSpec((tm, tn), lambda j, i: (i, j)),
        scratch_shapes=[pltpu.VMEM((k, tn), BF16)],
        compiler_params=_params(("arbitrary", "arbitrary")),
        name="matmul",
    )(*args)


def _layer_norm_rows(z, g, b):
    mu = jnp.mean(z, axis=-1, keepdims=True)
    zc = z - mu
    var = jnp.mean(zc * zc, axis=-1, keepdims=True)
    return zc * lax.rsqrt(var + LN_EPS) * g + b


def _pack_bf16_halves(y):
    h = y.shape[1] // 2
    lo = lax.bitcast_convert_type(y[:, :h].astype(BF16).astype(F32), jnp.uint32)
    hi = lax.bitcast_convert_type(y[:, h:].astype(BF16).astype(F32), jnp.uint32)
    return hi | (lo >> 16)


def _unpack_bf16_halves(u):
    lo = lax.bitcast_convert_type(u << 16, F32).astype(BF16)
    hi = lax.bitcast_convert_type(u & jnp.uint32(0xFFFF0000), F32).astype(BF16)
    return lo, hi


def _ln_res_kernel(x_ref, m_ref, g_ref, b_ref, o_ref, obf_ref, opk_ref, *, alpha):
    y = _layer_norm_rows(alpha * x_ref[...] + m_ref[...], g_ref[...], b_ref[...])
    o_ref[...] = y
    obf_ref[...] = y.astype(BF16)
    opk_ref[...] = _pack_bf16_halves(y)


def _vec_spec(d, layer, ngrid):
    if ngrid == 1:
        return pl.BlockSpec((None, 1, d), lambda i: (layer, 0, 0))
    return pl.BlockSpec((None, 1, d), lambda i, j: (layer, 0, 0))


def ln_residual(x, m, g, b, layer, alpha):
    t, d = x.shape
    tm = _tile(t, (256, 128, 64, 32, 16))
    row = pl.BlockSpec((tm, d), lambda i: (i, 0))
    vec = _vec_spec(d, layer, 1)
    return pl.pallas_call(
        functools.partial(_ln_res_kernel, alpha=alpha),
        out_shape=(jax.ShapeDtypeStruct((t, d), F32), jax.ShapeDtypeStruct((t, d), BF16),
                   jax.ShapeDtypeStruct((t, d // 2), jnp.uint32)),
        grid=(t // tm,),
        in_specs=[row, row, vec, vec],
        out_specs=(row, row, pl.BlockSpec((tm, d // 2), lambda i: (i, 0))),
        compiler_params=_params(("arbitrary",)),
        name="ln_residual",
    )(x, m, g.reshape(g.shape[0], 1, d), b.reshape(b.shape[0], 1, d))


def _hgrn_head(hq, hf, v, hg, lb, gn, st, chunk):
    rb, dk = hq.shape
    nc = rb // chunk
    n_sub = max(chunk // SUBLANES, 1)
    sub = chunk // n_sub
    log_lb = jnp.log(lb)
    log_1m_lb = jnp.log1p(-lb)
    q = hq * _sigmoid(hq)
    log_sig = jnp.minimum(hf, 0.0) - jnp.log1p(jnp.exp(-jnp.abs(hf)))
    bt = log_1m_lb + log_sig
    log_f = jnp.maximum(log_lb, bt) + jnp.log1p(jnp.exp(-jnp.abs(log_lb - bt)))
    k = (1.0 - lb) * _sigmoid(-hf)

    tril = (lax.broadcasted_iota(jnp.int32, (chunk, chunk), 0)
            >= lax.broadcasted_iota(jnp.int32, (chunk, chunk), 1)).astype(F32)
    if nc > 1:
        logf_w = jnp.concatenate([log_f[c * chunk:(c + 1) * chunk, :] for c in range(nc)], axis=1)
    else:
        logf_w = log_f
    cum_w = jnp.dot(tril, logf_w, precision=lax.Precision.HIGHEST, preferred_element_type=F32)
    if nc > 1:
        cum = jnp.concatenate([cum_w[:, c * dk:(c + 1) * dk] for c in range(nc)], axis=0)
    else:
        cum = cum_w
    cum3 = cum.reshape(nc, chunk, dk)
    q3 = q.reshape(nc, chunk, dk)
    k3 = k.reshape(nc, chunk, dk)

    lane_t = lax.broadcasted_iota(jnp.int32, (nc, sub, chunk), 2)
    s_in_sub = lax.broadcasted_iota(jnp.int32, (nc, sub, dk), 1)
    att_t = [jnp.zeros((nc, sub, chunk), F32) for _ in range(n_sub)]
    cum3_log2 = cum3 * LOG2_E
    for t in range(chunk):
        jt = t // sub
        qt = q3[:, t:t + 1, :]
        ct = cum3_log2[:, t:t + 1, :]
        for j in range(jt + 1):
            rel = ct - cum3_log2[:, j * sub:(j + 1) * sub, :]
            if j == jt:
                rel = jnp.where(s_in_sub <= (t - jt * sub), rel, -jnp.inf)
            p = (qt * jnp.exp2(rel)) * k3[:, j * sub:(j + 1) * sub, :]
            r = jnp.sum(p, axis=-1, keepdims=True)
            att_t[j] = jnp.where(lane_t == t, r, att_t[j])

    last3 = cum3[:, chunk - 1:chunk, :]
    qe = (q * jnp.exp(cum)).astype(BF16)
    kdec = (k3 * jnp.exp(last3 - cum3)).reshape(rb, dk).astype(BF16)
    vb = v.astype(BF16)
    tn_dims = (((0,), (0,)), ((), ()))
    nt_dims = (((1,), (1,)), ((), ()))
    o_parts = []
    for c in range(nc):
        rows = slice(c * chunk, (c + 1) * chunk)
        att_c = att_t[0][c] if n_sub == 1 else jnp.concatenate([a[c] for a in att_t], axis=0)
        o_c = lax.dot_general(qe[rows], st.astype(BF16), nt_dims, preferred_element_type=F32)
        o_c = o_c + lax.dot_general(att_c.astype(BF16), vb[rows], tn_dims, preferred_element_type=F32)
        inc = lax.dot_general(vb[rows], kdec[rows], tn_dims, preferred_element_type=F32)
        st = jnp.exp(cum[(c + 1) * chunk - 1:(c + 1) * chunk, :]) * st + inc
        o_parts.append(o_c)
    o = o_parts[0] if nc == 1 else jnp.concatenate(o_parts, axis=0)
    o = o * lax.rsqrt(jnp.mean(o * o, axis=-1, keepdims=True) + RMS_EPS) * gn * (hg * _sigmoid(hg))
    return o, st


def _hgrn_kernel(*refs, chunk, has_s0):
    if has_s0:
        q_ref, f_ref, i_ref, g_ref, lb_ref, gn_ref, s0_ref, o_ref, sout_ref, st_ref = refs
    else:
        q_ref, f_ref, i_ref, g_ref, lb_ref, gn_ref, o_ref, sout_ref, st_ref = refs
    c_id = pl.program_id(2)
    hb, dv, dk = st_ref.shape

    @pl.when(c_id == 0)
    def _():
        for hh in range(hb):
            st_ref[hh] = s0_ref[hh].T if has_s0 else jnp.zeros((dv, dk), F32)

    states = []
    for hh in range(hb):
        cs = slice(hh * dk, (hh + 1) * dk)
        o, st = _hgrn_head(q_ref[:, cs], f_ref[:, cs], i_ref[:, cs], g_ref[:, cs], lb_ref[:, cs], gn_ref[...],
                           st_ref[hh], chunk)
        st_ref[hh] = st
        o_ref[:, cs] = o.astype(o_ref.dtype)
        states.append(st)

    @pl.when(c_id == pl.num_programs(2) - 1)
    def _():
        for hh in range(hb):
            sout_ref[hh] = states[hh].T


def hgrn_recurrence(hcat, lb, gnorm, s0, o_prev, *, row0, nb, seq, chunk):
    t, d4 = hcat.shape
    d = d4 // 4
    hd = HG_EXPAND
    nh = d // hd
    rb = max(_tile(seq, (512, 256, 128, 64, 32, 16, 8)), chunk)
    nrb = seq // rb
    rblk0 = row0 // rb
    hb = HG_HEADS_PER_STEP if (rb == chunk and nh % HG_HEADS_PER_STEP == 0) else 1
    cw = hb * hd
    ngrp = nh // hb

    def col(off):
        return pl.BlockSpec((rb, cw), lambda b, h, c, off=off: (rblk0 + b * nrb + c, off + h))

    in_specs = [col(0), col(ngrp), col(2 * ngrp), col(3 * ngrp),
                pl.BlockSpec((1, cw), lambda b, h, c: (0, h)),
                pl.BlockSpec((1, hd), lambda b, h, c: (0, 0))]
    args = [hcat, hcat, hcat, hcat, lb.reshape(1, d), gnorm.reshape(1, hd)]
    if s0 is not None:
        in_specs.append(pl.BlockSpec((None, hb, hd, hd), lambda b, h, c: (b, h, 0, 0)))
        args.append(s0)
    aliases = {}
    if o_prev is not None:
        in_specs.append(pl.BlockSpec(memory_space=pl.ANY))
        args.append(o_prev)
        aliases = {len(args) - 1: 0}

    def kern(*refs):
        if o_prev is not None:
            n_in = len(args)
            refs = refs[:n_in - 1] + refs[n_in:]
        _hgrn_kernel(*refs, chunk=chunk, has_s0=s0 is not None)

    return pl.pallas_call(
        kern,
        out_shape=(jax.ShapeDtypeStruct((t, d), F32), jax.ShapeDtypeStruct((nb, nh, hd, hd), F32)),
        grid=(nb, ngrp, nrb),
        in_specs=in_specs,
        out_specs=(pl.BlockSpec((rb, cw), lambda b, h, c: (rblk0 + b * nrb + c, h)),
                   pl.BlockSpec((None, hb, hd, hd), lambda b, h, c: (b, h, 0, 0))),
        scratch_shapes=[pltpu.VMEM((hb, hd, hd), F32)],
        input_output_aliases=aliases,
        compiler_params=_params(("arbitrary", "arbitrary", "arbitrary")),
        name="hgrn_recurrence",
    )(*args)


def _swa_kernel(q_ref, kp_ref, ko_ref, vp_ref, vo_ref, sink_ref, o_ref, *, group, first_block_has_no_prev):
    lq = q_ref.shape[0]
    sp_len = kp_ref.shape[0]
    so_len = ko_ref.shape[0]
    hd = SWA_HEAD_DIM
    scale = hd ** -0.5
    i_p = lax.broadcasted_iota(jnp.int32, (lq, sp_len), 0)
    j_p = lax.broadcasted_iota(jnp.int32, (lq, sp_len), 1)
    mask_p = j_p > i_p
    if first_block_has_no_prev:
        mask_p = jnp.logical_and(mask_p, pl.program_id(1) > 0)
    i_o = lax.broadcasted_iota(jnp.int32, (lq, so_len), 0)
    j_o = lax.broadcasted_iota(jnp.int32, (lq, so_len), 1)
    mask_o = j_o <= i_o
    nt = (((1,), (1,)), ((), ()))
    kv_per_block = kp_ref.shape[1] // hd
    heads = [(kv, kv * group + g) for kv in range(kv_per_block) for g in range(group)]
    kvs = []
    for kv in range(kv_per_block):
        cs = slice(kv * hd, (kv + 1) * hd)
        kvs.append((kp_ref[:, cs].astype(BF16), ko_ref[:, cs].astype(BF16),
                    vp_ref[:, cs].astype(BF16), vo_ref[:, cs].astype(BF16)))
    scores = []
    for kv, h in heads:
        qh = q_ref[:, h * hd:(h + 1) * hd].astype(BF16)
        s_p = lax.dot_general(qh, kvs[kv][0], nt, preferred_element_type=F32) * scale
        s_o = lax.dot_general(qh, kvs[kv][1], nt, preferred_element_type=F32) * scale
        scores.append((jnp.where(mask_p, s_p, -jnp.inf), jnp.where(mask_o, s_o, -jnp.inf)))
    probs = []
    for (kv, h), (s_p, s_o) in zip(heads, scores):
        sink = sink_ref[0:1, h:h + 1]
        m = jnp.maximum(jnp.maximum(jnp.max(s_p, axis=-1, keepdims=True),
                                    jnp.max(s_o, axis=-1, keepdims=True)), sink)
        p_p = jnp.exp(s_p - m)
        p_o = jnp.exp(s_o - m)
        den = (jnp.sum(p_p, axis=-1, keepdims=True) + jnp.sum(p_o, axis=-1, keepdims=True)
               + jnp.exp(sink - m))
        probs.append(((p_p / den).astype(BF16), (p_o / den).astype(BF16)))
    outs = []
    for (kv, h), (p_p, p_o) in zip(heads, probs):
        outs.append(jnp.dot(p_p, kvs[kv][2], preferred_element_type=F32)
                    + jnp.dot(p_o, kvs[kv][3], preferred_element_type=F32))
    for (kv, h), o in zip(heads, outs):
        o_ref[:, h * hd:(h + 1) * hd] = o.astype(o_ref.dtype)


def swa_attention(qkv, k_prev, v_prev, sinks, o_prev, *, row0, nb, seq, d):
    t = qkv.shape[0]
    hd = SWA_HEAD_DIM
    group = d // hd // SWA_KV_HEADS
    kvb = LANES // hd
    n_kvb = SWA_KV_HEADS // kvb
    qw = kvb * group * hd
    lq = min(seq, SWA_WINDOW)
    nq = seq // lq
    rblk0 = row0 // lq
    kcol0 = d // LANES
    vcol0 = (d + SWA_KV_HEADS * hd) // LANES
    is_prompt = k_prev is None
    own = lambda c0: pl.BlockSpec((lq, LANES), lambda b, n, p: (rblk0 + b * nq + n, c0 + p))
    if is_prompt:
        prev = lambda c0: pl.BlockSpec(
            (lq, LANES), lambda b, n, p: (rblk0 + b * nq + jnp.maximum(n - 1, 0), c0 + p))
        kp_spec, vp_spec = prev(kcol0), prev(vcol0)
        k_prev = v_prev = qkv
    else:
        kp_spec = vp_spec = pl.BlockSpec((SWA_WINDOW, LANES), lambda b, n, p: (b, p))
    in_specs = [pl.BlockSpec((lq, qw), lambda b, n, p: (rblk0 + b * nq + n, p)),
                kp_spec, own(kcol0), vp_spec, own(vcol0),
                pl.BlockSpec((None, 1, kvb * group), lambda b, n, p: (p, 0, 0))]
    args = [qkv, k_prev, qkv, v_prev, qkv, sinks.reshape(n_kvb, 1, kvb * group)]
    aliases = {}
    if o_prev is not None:
        in_specs.append(pl.BlockSpec(memory_space=pl.ANY))
        args.append(o_prev)
        aliases = {len(args) - 1: 0}

    def kern(*refs):
        if o_prev is not None:
            n_in = len(args)
            refs = refs[:n_in - 1] + refs[n_in:]
        _swa_kernel(*refs, group=group, first_block_has_no_prev=is_prompt)

    return pl.pallas_call(
        kern,
        out_shape=jax.ShapeDtypeStruct((t, d), F32),
        grid=(nb, nq, n_kvb),
        in_specs=in_specs,
        out_specs=pl.BlockSpec((lq, qw), lambda b, n, p: (rblk0 + b * nq + n, p)),
        input_output_aliases=aliases,
        compiler_params=_params(("arbitrary", "arbitrary", "arbitrary")),
        name="swa_attention",
    )(*args)


CONV_HALO = 32


def _conv_kernel(*refs, width, raw_halo):
    if raw_halo:
        a_ref, gt_ref, ha_ref, hgt_ref, wdw_ref, bdw_ref, lng_ref, lnb_ref, y_ref, u_ref, full_ref, acc_ref = refs
    else:
        a_ref, gt_ref, hu_ref, wdw_ref, bdw_ref, lng_ref, lnb_ref, y_ref, u_ref, full_ref, acc_ref = refs
    rb, c = a_ref.shape
    u = a_ref[...] * _sigmoid(gt_ref[...])
    u_ref[...] = u
    if raw_halo:
        hu = ha_ref[...] * _sigmoid(hgt_ref[...])
        hu = jnp.where(pl.program_id(1) > 0, hu, 0.0)
    else:
        hu = hu_ref[...]
    full_ref[0:CONV_HALO, :] = hu
    full_ref[CONV_HALO:CONV_HALO + rb, :] = u
    lead = CONV_HALO - (width - 1)
    rt = min(rb, 64)

    def col_tile(ct, carry):
        c0 = pl.multiple_of(ct * LANES, LANES)
        for r0 in range(0, rb, rt):
            acc = jnp.broadcast_to(bdw_ref[:, pl.ds(c0, LANES)], (rt, LANES))
            for w in range(width):
                acc = acc + full_ref[r0 + lead + w:r0 + lead + w + rt, pl.ds(c0, LANES)] * wdw_ref[w:w + 1, pl.ds(c0, LANES)]
            acc_ref[r0:r0 + rt, pl.ds(c0, LANES)] = acc
        return carry

    lax.fori_loop(0, c // LANES, col_tile, 0)
    y = _layer_norm_rows(acc_ref[...], lng_ref[...], lnb_ref[...])
    y_ref[...] = (y * _sigmoid(y)).astype(y_ref.dtype)


def conv_block(hcat, hist, w_dw, b_dw, ln_g, ln_b, layer, y_prev, u_prev, *, row0, nb, seq):
    t, c2 = hcat.shape
    c = c2 // 2
    nl, width, _ = w_dw.shape
    rb = _tile(seq, (256, 128, 64, 32, 16, 8))
    nrb = seq // rb
    rblk0 = row0 // rb
    raw_halo = hist is None
    main = lambda half: pl.BlockSpec((rb, c), lambda b, n: (rblk0 + b * nrb + n, half))
    vec = _vec_spec(c, layer, 2)
    in_specs = [main(0), main(1)]
    args = [hcat, hcat]
    if raw_halo:
        hpb = rb // CONV_HALO
        halo = lambda half: pl.BlockSpec(
            (CONV_HALO, c), lambda b, n: (jnp.maximum((rblk0 + b * nrb + n) * hpb - 1, 0), half))
        in_specs += [halo(0), halo(1)]
        args += [hcat, hcat]
    else:
        in_specs.append(pl.BlockSpec((None, CONV_HALO, c), lambda b, n: (b, 0, 0)))
        args.append(hist)
    in_specs += [pl.BlockSpec((None, width, c), lambda b, n: (layer, 0, 0)), vec, vec, vec]
    args += [w_dw, b_dw.reshape(nl, 1, c), ln_g.reshape(nl, 1, c), ln_b.reshape(nl, 1, c)]
    n_real = len(args)
    aliases = {}
    if y_prev is not None:
        in_specs += [pl.BlockSpec(memory_space=pl.ANY), pl.BlockSpec(memory_space=pl.ANY)]
        args += [y_prev, u_prev]
        aliases = {n_real: 0, n_real + 1: 1}

    def kern(*refs):
        if y_prev is not None:
            refs = refs[:n_real] + refs[n_real + 2:]
        _conv_kernel(*refs, width=width, raw_halo=raw_halo)

    out_blk = pl.BlockSpec((rb, c), lambda b, n: (rblk0 + b * nrb + n, 0))
    return pl.pallas_call(
        kern,
        out_shape=(jax.ShapeDtypeStruct((t, c), F32), jax.ShapeDtypeStruct((t, c), F32)),
        grid=(nb, nrb),
        in_specs=in_specs,
        out_specs=(out_blk, out_blk),
        scratch_shapes=[pltpu.VMEM((CONV_HALO + rb, c), F32), pltpu.VMEM((rb, c), F32)],
        input_output_aliases=aliases,
        compiler_params=_params(("arbitrary", "arbitrary")),
        name="conv_block",
    )(*args)


def _router_kernel(x_ref, w_ref, b_ref, eidx_ref, gate_ref, rank_ref, cnt_ref, run_ref):
    @pl.when(pl.program_id(0) == 0)
    def _():
        run_ref[...] = jnp.zeros_like(run_ref)

    logits = jnp.dot(x_ref[...], w_ref[...], precision=lax.Precision.HIGHEST, preferred_element_type=F32)
    scores = _sigmoid(logits)
    sel = scores + b_ref[...]
    tm, ne = sel.shape
    gsz = ne // MOE_GROUPS
    lane = lax.broadcasted_iota(jnp.int32, (tm, ne), 1).astype(F32)
    grp = lax.broadcasted_iota(jnp.int32, (tm, ne), 1) // gsz
    grp_f = grp.astype(F32)
    neg = -jnp.inf

    def first_argmax(v, idx_f):
        m = jnp.max(v, axis=-1, keepdims=True)
        return m, jnp.min(jnp.where(v == m, idx_f, float(ne)), axis=-1, keepdims=True)

    gscore = jnp.zeros((tm, ne), F32)
    for gi in range(MOE_GROUPS):
        in_g = grp == gi
        v = jnp.where(in_g, sel, neg)
        m1, i1 = first_argmax(v, lane)
        m2 = jnp.max(jnp.where(lane == i1, neg, v), axis=-1, keepdims=True)
        gscore = jnp.where(in_g, m1 + m2, gscore)
    gmask = jnp.zeros((tm, ne), jnp.bool_)
    for _ in range(MOE_TOPK_GROUPS):
        _, gi = first_argmax(gscore, grp_f)
        hit = grp_f == gi
        gmask = jnp.logical_or(gmask, hit)
        gscore = jnp.where(hit, neg, gscore)
    cand = jnp.where(gmask, sel, neg)
    k_iota = lax.broadcasted_iota(jnp.int32, (tm, MOE_TOP_K), 1)
    eidx = jnp.zeros((tm, MOE_TOP_K), F32)
    gate = jnp.zeros((tm, MOE_TOP_K), F32)
    chosen = jnp.zeros((tm, ne), F32)
    picks = []
    for kk in range(MOE_TOP_K):
        _, ei = first_argmax(cand, lane)
        hit = lane == ei
        wk = jnp.sum(jnp.where(hit, scores, 0.0), axis=-1, keepdims=True)
        cand = jnp.where(hit, neg, cand)
        chosen = jnp.where(hit, 1.0, chosen)
        eidx = jnp.where(k_iota == kk, ei, eidx)
        gate = jnp.where(k_iota == kk, wk, gate)
        picks.append(hit)
    gate = gate / jnp.sum(gate, axis=-1, keepdims=True) * MOE_ROUTED_SCALE
    eidx_ref[...] = eidx.astype(jnp.int32)
    gate_ref[...] = gate
    earlier = (lax.broadcasted_iota(jnp.int32, (tm, tm), 0)
               > lax.broadcasted_iota(jnp.int32, (tm, tm), 1)).astype(BF16)
    before = jnp.dot(earlier, chosen.astype(BF16), preferred_element_type=F32) + run_ref[...]
    rank = jnp.zeros((tm, MOE_TOP_K), F32)
    for kk in range(MOE_TOP_K):
        rk = jnp.sum(jnp.where(picks[kk], before, 0.0), axis=-1, keepdims=True)
        rank = jnp.where(k_iota == kk, rk, rank)
    rank_ref[...] = rank.astype(jnp.int32)
    run = run_ref[...] + jnp.sum(chosen, axis=0, keepdims=True)
    run_ref[...] = run
    cnt_ref[...] = run.astype(jnp.int32)


def moe_router(x, w_router, b_router, layer):
    t, d = x.shape
    nl, _, ne = w_router.shape
    tm = _tile(t, (256, 128, 64, 32, 16, 8))
    tok = lambda dt: jax.ShapeDtypeStruct((t, MOE_TOP_K), dt)
    tok_spec = pl.BlockSpec((tm, MOE_TOP_K), lambda i: (i, 0))
    return pl.pallas_call(
        _router_kernel,
        out_shape=(tok(jnp.int32), tok(F32), tok(jnp.int32), jax.ShapeDtypeStruct((1, ne), jnp.int32)),
        grid=(t // tm,),
        in_specs=[pl.BlockSpec((tm, d), lambda i: (i, 0)),
                  pl.BlockSpec((None, d, ne), lambda i: (layer, 0, 0)),
                  pl.BlockSpec((None, 1, ne), lambda i: (layer, 0, 0))],
        out_specs=(tok_spec, tok_spec, tok_spec, pl.BlockSpec((1, ne), lambda i: (0, 0))),
        scratch_shapes=[pltpu.VMEM((1, ne), F32)],
        compiler_params=_params(("arbitrary",)),
        name="moe_router",
    )(x, w_router, b_router.reshape(nl, 1, ne))


def _pad_pieces(tm, zr):
    pieces = [zr] * ((tm - 1) // zr)
    s = zr // 2
    while s >= SUBLANES:
        pieces.append(s)
        s //= 2
    return pieces


def _dispatch_kernel(padlo_ref, padn_ref, pos_ref, x_ref, o_hbm, zero_ref, sem, zsem, *, tm):
    n = pos_ref.shape[-1]

    def row_copy(t, dst):
        return pltpu.make_async_copy(x_ref.at[pl.ds(t, 1), :], o_hbm.at[pl.ds(dst, 1), :], sem)

    def issue(t, carry):
        for kk in range(MOE_TOP_K):
            row_copy(t, pos_ref[0, t * MOE_TOP_K + kk]).start()
        return carry

    lax.fori_loop(0, n // MOE_TOP_K, issue, 0)

    @pl.when(pl.program_id(0) == 0)
    def _():
        zero_ref[...] = jnp.zeros_like(zero_ref)

        def fill(e, wait):
            def piece(take, off, s):
                cp = pltpu.make_async_copy(zero_ref.at[pl.ds(0, s), :], o_hbm.at[pl.ds(off, s), :], zsem)

                @pl.when(take)
                def _():
                    if wait:
                        cp.wait()
                    else:
                        cp.start()

            off = padlo_ref[e]
            left = padn_ref[e]
            head = jnp.minimum((-off) & (SUBLANES - 1), left)
            for r in range(SUBLANES - 1):
                piece(head > r, off + r, 1)
            off = off + head
            left = left - head
            for s in _pad_pieces(tm, zero_ref.shape[0]):
                take = left >= s
                piece(take, pl.multiple_of(off, SUBLANES), s)
                step = jnp.where(take, s, 0)
                off = off + step
                left = left - step

        def fill_start(e, carry):
            fill(e, False)
            return carry

        def fill_wait(e, carry):
            fill(e, True)
            return carry

        ne = padlo_ref.shape[0]
        lax.fori_loop(0, ne, fill_start, 0)
        lax.fori_loop(0, ne, fill_wait, 0)

    def drain(t, carry):
        for _ in range(MOE_TOP_K):
            row_copy(t, 0).wait()
        return carry

    lax.fori_loop(0, n // MOE_TOP_K, drain, 0)


ZERO_ROWS = 32


def dispatch_rows(x, pos, pad_lo, pad_n, n_rows, tm):
    t, d = x.shape
    tt = _tile(t, (128, 64, 32, 16, 8))
    n_t = t // tt
    grid_spec = pltpu.PrefetchScalarGridSpec(
        num_scalar_prefetch=2,
        grid=(n_t,),
        in_specs=[pl.BlockSpec((None, 1, tt * MOE_TOP_K), lambda i, lo, nn: (i, 0, 0), memory_space=pltpu.SMEM),
                  pl.BlockSpec((tt, d), lambda i, lo, nn: (i, 0))],
        out_specs=pl.BlockSpec(memory_space=pl.ANY),
        scratch_shapes=[pltpu.VMEM((min(ZERO_ROWS, tm), d), x.dtype),
                        pltpu.SemaphoreType.DMA(()), pltpu.SemaphoreType.DMA(())],
    )
    return pl.pallas_call(
        functools.partial(_dispatch_kernel, tm=tm),
        out_shape=jax.ShapeDtypeStruct((n_rows, d), x.dtype),
        grid_spec=grid_spec,
        compiler_params=_params(("arbitrary",)),
        name="moe_dispatch",
    )(pad_lo, pad_n, pos.reshape(n_t, 1, tt * MOE_TOP_K), x)


def _expert_gu_kernel(e_ref, f_ref, blk_ref, first_ref, valid_ref, x_ref, wg_ref, wu_ref, h_ref, wcat_ref):
    i = pl.program_id(0)
    tf = wg_ref.shape[1]

    @pl.when(jnp.logical_and(valid_ref[i] == 1, first_ref[i] == 1))
    def _():
        wcat_ref[:, 0:tf] = wg_ref[...].astype(BF16)
        wcat_ref[:, tf:2 * tf] = wu_ref[...].astype(BF16)

    @pl.when(valid_ref[i] == 1)
    def _():
        if x_ref.dtype == jnp.uint32:
            half = x_ref.shape[1]
            x_lo, x_hi = _unpack_bf16_halves(x_ref[...])
            gu = (jnp.dot(x_lo, wcat_ref[0:half, :], preferred_element_type=F32)
                  + jnp.dot(x_hi, wcat_ref[half:2 * half, :], preferred_element_type=F32))
        else:
            gu = jnp.dot(x_ref[...].astype(BF16), wcat_ref[...], preferred_element_type=F32)
        g = gu[:, 0:tf]
        u = gu[:, tf:2 * tf]
        h_ref[...] = (g * _sigmoid(g) * u).astype(h_ref.dtype)


def expert_gate_up(x_rows, w_gu, layer, item_e, item_f, item_blk, item_first, item_valid, tm, tf):
    n_rows, xw = x_rows.shape
    d = w_gu.shape[2]
    ff = w_gu.shape[3] // 2
    n_items = item_e.shape[0]
    nf = ff // tf
    grid_spec = pltpu.PrefetchScalarGridSpec(
        num_scalar_prefetch=5,
        grid=(n_items,),
        in_specs=[pl.BlockSpec((tm, xw), lambda i, e, f, b, fi, va: (b[i], 0)),
                  pl.BlockSpec((None, None, d, tf), lambda i, e, f, b, fi, va: (layer, e[i], 0, f[i])),
                  pl.BlockSpec((None, None, d, tf), lambda i, e, f, b, fi, va: (layer, e[i], 0, nf + f[i]))],
        out_specs=pl.BlockSpec((tm, tf), lambda i, e, f, b, fi, va: (b[i], f[i])),
        scratch_shapes=[pltpu.VMEM((d, 2 * tf), BF16)],
    )
    return pl.pallas_call(
        _expert_gu_kernel,
        out_shape=jax.ShapeDtypeStruct((n_rows, ff), BF16),
        grid_spec=grid_spec,
        compiler_params=_params(("arbitrary",)),
        name="expert_gate_up",
    )(item_e, item_f, item_blk, item_first, item_valid, x_rows, w_gu, w_gu)


def _expert_dn_kernel(e_ref, first_ref, valid_ref, blk_ref, h_ref, w_ref, y_ref, wb_ref):
    i = pl.program_id(0)

    @pl.when(jnp.logical_and(valid_ref[i] == 1, first_ref[i] == 1))
    def _():
        wb_ref[...] = w_ref[...].astype(BF16)

    @pl.when(valid_ref[i] == 1)
    def _():
        y_ref[...] = jnp.dot(h_ref[...], wb_ref[...], preferred_element_type=F32)


def expert_down(h_rows, w_dn, layer, blk_e, blk_first, blk_valid, blk_idx, tm):
    n_rows, ff = h_rows.shape
    d = w_dn.shape[3]
    n_blk = blk_e.shape[0]
    grid_spec = pltpu.PrefetchScalarGridSpec(
        num_scalar_prefetch=4,
        grid=(n_blk,),
        in_specs=[pl.BlockSpec((tm, ff), lambda i, e, fi, va, b: (b[i], 0)),
                  pl.BlockSpec((None, None, ff, d), lambda i, e, fi, va, b: (layer, e[i], 0, 0))],
        out_specs=pl.BlockSpec((tm, d), lambda i, e, fi, va, b: (b[i], 0)),
        scratch_shapes=[pltpu.VMEM((ff, d), BF16)],
    )
    return pl.pallas_call(
        _expert_dn_kernel,
        out_shape=jax.ShapeDtypeStruct((n_rows, d), F32),
        grid_spec=grid_spec,
        compiler_params=_params(("arbitrary",)),
        name="expert_down",
    )(blk_e, blk_first, blk_valid, blk_idx, h_rows, w_dn)


def _combine_kernel(pos_ref, pos_next_ref, gate_ref, y_hbm, ysh_ref, x_ref, g_ref, b_ref, o_ref, obf_ref,
                    buf_ref, sem, *, alpha):
    tc = x_ref.shape[0]
    i = pl.program_id(0)
    slot = i % 2

    def row_copy(s, t, kk, src):
        return pltpu.make_async_copy(y_hbm.at[pl.ds(src, 1), :], buf_ref.at[s, kk, pl.ds(t, 1), :], sem.at[s])

    def issue(p_ref, s):
        def body(t, carry):
            for kk in range(MOE_TOP_K):
                row_copy(s, t, kk, p_ref[0, t * MOE_TOP_K + kk]).start()
            return carry

        lax.fori_loop(0, tc, body, 0)

    @pl.when(i == 0)
    def _():
        issue(pos_ref, 0)

    @pl.when(i + 1 < pl.num_programs(0))
    def _():
        issue(pos_next_ref, 1 - slot)

    def drain(t, carry):
        for kk in range(MOE_TOP_K):
            row_copy(slot, t, kk, 0).wait()
        return carry

    lax.fori_loop(0, tc, drain, 0)
    y = ysh_ref[...]
    gate = gate_ref[...]
    for kk in range(MOE_TOP_K):
        y = y + gate[:, kk:kk + 1] * buf_ref[slot, kk]
    out = _layer_norm_rows(alpha * x_ref[...] + y, g_ref[...], b_ref[...])
    o_ref[...] = out
    obf_ref[...] = out.astype(BF16)


def moe_combine(pos, gate, y_rows, y_shared, x, g, b, layer, alpha):
    t, d = x.shape
    tc = _tile(t, (32, 16))
    n_t = t // tc
    row = pl.BlockSpec((tc, d), lambda i: (i, 0))
    vec = _vec_spec(d, layer, 1)
    pos_tiles = pos.reshape(n_t, 1, tc * MOE_TOP_K)
    return pl.pallas_call(
        functools.partial(_combine_kernel, alpha=alpha),
        out_shape=(jax.ShapeDtypeStruct((t, d), F32), jax.ShapeDtypeStruct((t, d), BF16)),
        grid=(n_t,),
        in_specs=[pl.BlockSpec((None, 1, tc * MOE_TOP_K), lambda i: (i, 0, 0), memory_space=pltpu.SMEM),
                  pl.BlockSpec((None, 1, tc * MOE_TOP_K), lambda i: (jnp.minimum(i + 1, n_t - 1), 0, 0),
                               memory_space=pltpu.SMEM),
                  pl.BlockSpec((tc, MOE_TOP_K), lambda i: (i, 0)),
                  pl.BlockSpec(memory_space=pl.ANY),
                  row, row, vec, vec],
        out_specs=(row, row),
        scratch_shapes=[pltpu.VMEM((2, MOE_TOP_K, tc, d), F32), pltpu.SemaphoreType.DMA((2,))],
        compiler_params=_params(("arbitrary",)),
        name="moe_combine",
    )(pos_tiles, pos_tiles, gate, y_rows, y_shared, x,
      g.reshape(g.shape[0], 1, d), b.reshape(b.shape[0], 1, d))


def _moe_plan(eidx, rank, counts, tm, nf):
    t, k = eidx.shape
    n_exp = counts.shape[0]
    n_blk = -(-(t * k) // tm) + n_exp
    nb_e = (counts + tm - 1) // tm
    blk_end = jnp.cumsum(nb_e)
    blk_start = blk_end - nb_e
    n_used = blk_end[-1]
    row0 = blk_start * tm
    pos = _lookup(row0, eidx) + rank
    blk = jnp.arange(n_blk, dtype=jnp.int32)
    blk_c = jnp.minimum(blk, n_used - 1)
    blk_e = jnp.sum((blk_c[:, None] >= blk_end[None, :]).astype(jnp.int32), axis=1)
    blk_valid = (blk < n_used).astype(jnp.int32)
    blk_first = (blk_c == _lookup(blk_start, blk_e)).astype(jnp.int32)
    n_items = n_blk * nf
    item_end = jnp.cumsum(nb_e * nf)
    it = jnp.minimum(jnp.arange(n_items, dtype=jnp.int32), item_end[-1] - 1)
    it_e = jnp.sum((it[:, None] >= item_end[None, :]).astype(jnp.int32), axis=1)
    local = it - _lookup(item_end - nb_e * nf, it_e)
    nbe = jnp.maximum(_lookup(nb_e, it_e), 1)
    it_f = local // nbe
    it_r = local % nbe
    it_blk = _lookup(blk_start, it_e) + it_r
    i32 = lambda a: a.astype(jnp.int32)
    return dict(pos=i32(pos), pad_lo=i32(row0 + counts), pad_n=i32(nb_e * tm - counts), n_rows=n_blk * tm,
                blk_e=i32(blk_e), blk_first=blk_first, blk_valid=blk_valid, blk_idx=i32(blk_c),
                it_e=i32(it_e), it_f=i32(it_f), it_blk=i32(it_blk), it_first=i32(it_r == 0),
                it_valid=i32(jnp.arange(n_items) < item_end[-1]))


def _ff_tile(ff):
    return _tile(ff, (384, 256, 128))


def moe_layer(x, x_bf, x_pk, layer, w_router, b_router, w_gu, w_dn, ws_gu, ws_dn, ln_g, ln_b, alpha):
    t, d = x.shape
    ff = w_dn.shape[2]
    tm = _tile(t, (256, 128, 64, 32, 16))
    tf = _ff_tile(ff)
    nf = ff // tf
    eidx, gate, rank, counts = moe_router(x, w_router, b_router, layer)
    plan = _moe_plan(eidx, rank, counts[0], tm, nf)
    x_rows = dispatch_rows(x_pk, plan['pos'], plan['pad_lo'], plan['pad_n'], plan['n_rows'], tm)
    h_rows = expert_gate_up(x_rows, w_gu, layer, plan['it_e'], plan['it_f'], plan['it_blk'], plan['it_first'],
                            plan['it_valid'], tm, tf)
    y_rows = expert_down(h_rows, w_dn, layer, plan['blk_e'], plan['blk_first'], plan['blk_valid'],
                         plan['blk_idx'], tm)
    sff = ws_dn.shape[1]
    stf = _ff_tile(sff)
    snf = sff // stf
    nsb = t // tm
    zeros = lambda n: jnp.zeros((n,), jnp.int32)
    ones = lambda n: jnp.ones((n,), jnp.int32)
    s_blk = jnp.tile(jnp.arange(nsb, dtype=jnp.int32), snf)
    s_f = jnp.repeat(jnp.arange(snf, dtype=jnp.int32), nsb)
    s_first = (s_blk == 0).astype(jnp.int32)
    hs = expert_gate_up(x_bf, ws_gu[:, None], layer, zeros(nsb * snf), s_f, s_blk, s_first, ones(nsb * snf), tm, stf)
    sb = jnp.arange(nsb, dtype=jnp.int32)
    ys = expert_down(hs, ws_dn[:, None], layer, zeros(nsb), (sb == 0).astype(jnp.int32), ones(nsb), sb, tm)
    return moe_combine(plan['pos'], gate, y_rows, ys, x, ln_g, ln_b, layer, alpha)


def kernel(x_prompt, x_sample, state_hgrn, cache_swa_k, cache_swa_v, state_conv, hg_w_in, hg_lb, hg_norm, hg_w_o, swa_w_qkv, swa_b_qkv, swa_sinks, swa_w_o, swa_b_o, cv_w_pw1, cv_b_pw1, cv_w_dw, cv_b_dw, cv_ln_g, cv_ln_b, cv_w_pw2, cv_b_pw2, ln_mix_g, ln_mix_b, ln_ffn_g, ln_ffn_b, moe_w_router, moe_b_router, moe_w_gu, moe_w_dn, moe_ws_gu, moe_ws_dn):
    bp, lp, d = x_prompt.shape
    bs, ls, _ = x_sample.shape
    tp, ts = bp * lp, bs * ls
    depth = ln_mix_g.shape[0]
    alpha = (2 * depth) ** 0.25
    kvd = SWA_KV_HEADS * SWA_HEAD_DIM
    buf_len = cache_swa_k.shape[2]
    cw = cv_w_dw.shape[1]

    lb_p = jax.nn.softmax(hg_lb.astype(F32), axis=0)
    lb_all = jnp.concatenate([jnp.zeros_like(lb_p[:1]), jnp.cumsum(lb_p[1:], axis=0)], axis=0)

    x = jnp.concatenate([x_prompt.reshape(tp, d), x_sample.reshape(ts, d)], axis=0)
    x_bf = x.astype(BF16)
    hg_p, hg_s, k_p, k_s, v_p, v_s, cv_p, cv_s = [], [], [], [], [], [], [], []
    for layer in range(depth):
        kind, j = layer % N_MIXERS, layer // N_MIXERS
        if kind == 0:
            hcat = matmul(x_bf, hg_w_in, j)
            o, sp = hgrn_recurrence(hcat, lb_all[j], hg_norm[j], None, None,
                                    row0=0, nb=bp, seq=lp, chunk=math.gcd(lp, HG_CHUNK))
            o, ss = hgrn_recurrence(hcat, lb_all[j], hg_norm[j], state_hgrn[j], o,
                                    row0=tp, nb=bs, seq=ls, chunk=math.gcd(ls, HG_CHUNK))
            mix = matmul(o, hg_w_o, j)
            hg_p.append(sp)
            hg_s.append(ss)
        elif kind == 1:
            qkv = matmul(x_bf, swa_w_qkv, j, swa_b_qkv)
            o = swa_attention(qkv, None, None, swa_sinks[j], None, row0=0, nb=bp, seq=lp, d=d)
            ck = cache_swa_k[j].reshape(bs * buf_len, kvd)
            cv = cache_swa_v[j].reshape(bs * buf_len, kvd)
            o = swa_attention(qkv, ck, cv, swa_sinks[j], o, row0=tp, nb=bs, seq=ls, d=d)
            mix = matmul(o, swa_w_o, j, swa_b_o)
            k_new = qkv[:, d:d + kvd]
            v_new = qkv[:, d + kvd:]
            tail = lambda a: a[:tp].reshape(bp, lp, SWA_KV_HEADS, SWA_HEAD_DIM)[:, lp - buf_len:]
            k_p.append(tail(k_new))
            v_p.append(tail(v_new))
            new = lambda a: a[tp:].reshape(bs, ls, SWA_KV_HEADS, SWA_HEAD_DIM)
            k_s.append(jnp.concatenate([cache_swa_k[j], new(k_new)], axis=1)[:, ls:])
            v_s.append(jnp.concatenate([cache_swa_v[j], new(v_new)], axis=1)[:, ls:])
        else:
            hcat = matmul(x_bf, cv_w_pw1, j, cv_b_pw1)
            cargs = (cv_w_dw, cv_b_dw, cv_ln_g, cv_ln_b, j)
            y, u = conv_block(hcat, None, *cargs, None, None, row0=0, nb=bp, seq=lp)
            keep = cw - 1
            hist = jnp.pad(state_conv[j], ((0, 0), (CONV_HALO - keep, 0), (0, 0)))
            y, u = conv_block(hcat, hist, *cargs, y, u, row0=tp, nb=bs, seq=ls)
            mix = matmul(y, cv_w_pw2, j, cv_b_pw2)
            cv_p.append(u[:tp].reshape(bp, lp, d)[:, lp - keep:])
            cv_s.append(jnp.concatenate([state_conv[j], u[tp:].reshape(bs, ls, d)], axis=1)[:, ls:])
        x, x_bf, x_pk = ln_residual(x, mix, ln_mix_g, ln_mix_b, layer, alpha)
        x, x_bf = moe_layer(x, x_bf, x_pk, layer, moe_w_router, moe_b_router, moe_w_gu, moe_w_dn,
                            moe_ws_gu, moe_ws_dn, ln_ffn_g, ln_ffn_b, alpha)
    return (x[:tp].reshape(bp, lp, d), x[tp:].reshape(bs, ls, d),
            jnp.stack(hg_p), jnp.stack(hg_s), jnp.stack(k_p), jnp.stack(k_s), jnp.stack(v_p), jnp.stack(v_s),
            jnp.stack(cv_p), jnp.stack(cv_s))
```
